```python
import math
import jax, jax.numpy as jnp
from jax import lax
import numpy as np

D_MODEL = 1024
BATCH = 4
SEQ = 8192
DEPTH = 4

ATT_HEADS = 8
ATT_HEAD_DIM = 64
ATT_V_DIM = 2 * ATT_HEAD_DIM
ROPE_DIM = ATT_HEAD_DIM // 4
ROPE_THETA = 500000.0
Q_BLOCK = 128
SSM_INNER = 2 * D_MODEL
SSM_HEAD_DIM = 64
SSM_HEADS = SSM_INNER // SSM_HEAD_DIM
SSM_GROUPS = 8
SSM_STATE = 128
SSM_CONV = 5
SSM_CHUNK = 128
DT_MIN = 0.001
DT_MAX = 0.1
PEER_HEADS = 8
PEER_KEYS = 128
PEER_EXPERTS = PEER_KEYS * PEER_KEYS
PEER_KEY_DIM = 256
PEER_HALF = PEER_KEY_DIM // 2
PEER_TOPK = 16
PEER_BLOCK = 128

RMS_EPS = 1e-6

Q_COLS = ATT_HEADS * 2 * ATT_HEAD_DIM
K_COLS = ATT_HEADS * 2 * ATT_HEAD_DIM
V_COLS = ATT_HEADS * ATT_V_DIM
Z_COLS = SSM_INNER
XBC_COLS = SSM_INNER + 2 * SSM_GROUPS * SSM_STATE
DT_COLS = 2 * SSM_HEADS
GATE_COLS = 2 * D_MODEL
IN_COLS = Q_COLS + K_COLS + V_COLS + Z_COLS + XBC_COLS + DT_COLS + GATE_COLS
SPLIT_POINTS = (Q_COLS,
                Q_COLS + K_COLS,
                Q_COLS + K_COLS + V_COLS,
                Q_COLS + K_COLS + V_COLS + Z_COLS,
                Q_COLS + K_COLS + V_COLS + Z_COLS + XBC_COLS,
                Q_COLS + K_COLS + V_COLS + Z_COLS + XBC_COLS + DT_COLS)

kernel_name = "hybrid_diffattn_ssd_peer_encoder"


def rmsnorm(x, g):
    xf = x.astype(jnp.float32)
    y = xf * lax.rsqrt(jnp.mean(xf * xf, axis=-1, keepdims=True) + RMS_EPS)
    return y.astype(x.dtype) * g


def rotary_tables(positions, dtype):
    inv_freq = ROPE_THETA ** (-jnp.arange(0, ROPE_DIM, 2, dtype=jnp.float32) / ROPE_DIM)
    ang = positions.astype(jnp.float32)[..., None] * inv_freq
    cos = jnp.cos(ang)[:, :, None, None, :].astype(dtype)
    sin = jnp.sin(ang)[:, :, None, None, :].astype(dtype)
    return cos, sin


def partial_rotary(t, cos, sin):
    half = ROPE_DIM // 2
    r1, r2, rest = t[..., :half], t[..., half:ROPE_DIM], t[..., ROPE_DIM:]
    return jnp.concatenate([r1 * cos - r2 * sin, r2 * cos + r1 * sin, rest], axis=-1)


def diff_attention(q, k, v, lam):
    b, s, h, _, d = q.shape
    scale = d ** -0.5
    nq = s // Q_BLOCK
    qb = q.reshape(b, nq, Q_BLOCK, h, 2, d).transpose(1, 0, 2, 3, 4, 5)

    def block(qblk):
        sc = jnp.einsum('bqhcd,bkhcd->bhcqk', qblk, k).astype(jnp.float32) * scale
        p = jax.nn.softmax(sc, axis=-1)
        w = p[:, :, 0] - lam * p[:, :, 1]
        return jnp.einsum('bhqk,bkhe->bqhe', w.astype(v.dtype), v)

    out = lax.map(block, qb)
    return out.transpose(1, 0, 2, 3, 4).reshape(b, s, h, 2 * d)


def centred_dwconv(u, w, bias):
    width, ch = w.shape
    out = lax.conv_general_dilated(
        u, w[:, None, :].astype(u.dtype), window_strides=(1,),
        padding=[(width // 2, width // 2)],
        dimension_numbers=('NWC', 'WIO', 'NWC'), feature_group_count=ch)
    return out + bias


def segsum(a):
    t = a.shape[-1]
    cs = jnp.cumsum(a, axis=-1)
    diff = cs[..., :, None] - cs[..., None, :]
    mask = jnp.tril(jnp.ones((t, t), dtype=bool))
    return jnp.where(mask, diff, -jnp.inf)


def ssd_scan(x, dt, a_neg, bm, cm):
    b, l, h, p = x.shape
    g, n = bm.shape[-2:]
    r = h // g
    c = l // SSM_CHUNK
    q = SSM_CHUNK
    xs = (x.astype(jnp.float32) * dt[..., None]).reshape(b, c, q, g, r, p)
    a = (dt * a_neg).reshape(b, c, q, g, r).transpose(0, 3, 4, 1, 2)
    bc = bm.reshape(b, c, q, g, n)
    cc = cm.reshape(b, c, q, g, n)
    a_cs = jnp.cumsum(a, axis=-1)
    decay_in = jnp.exp(segsum(a))
    cb = jnp.einsum('bclgn,bcsgn->bgcls', cc, bc)
    y_diag = jnp.einsum('bgrcls,bcsgrp->bclgrp', cb[:, :, None] * decay_in, xs)
    decay_states = jnp.exp(a_cs[..., -1:] - a_cs).transpose(0, 3, 4, 1, 2)
    states = jnp.einsum('bclgn,bclgrp->bcgrpn', bc, xs * decay_states[..., None])
    chunk_a = jnp.pad(a_cs[..., -1], ((0, 0), (0, 0), (0, 0), (1, 0)))
    decay_chunk = jnp.exp(segsum(chunk_a))
    states0 = jnp.concatenate([jnp.zeros_like(states[:, :1]), states], axis=1)
    entry = jnp.einsum('bgrzc,bcgrpn->bzgrpn', decay_chunk, states0)[:, :-1]
    decay_out = jnp.exp(a_cs).transpose(0, 3, 4, 1, 2)
    y_off = jnp.einsum('bclgn,bcgrpn->bclgrp', cc, entry) * decay_out[..., None]
    return (y_diag + y_off).reshape(b, l, h, p).astype(x.dtype)


def mamba2_bidir(z, xbc, dt_raw, conv_w, conv_b, dt_bias, a_log, d_skip, norm_g):
    b, s, _ = z.shape
    xbc = jax.nn.silu(centred_dwconv(xbc, conv_w, conv_b))
    xs, bm, cm = jnp.split(xbc, [SSM_INNER, SSM_INNER + SSM_GROUPS * SSM_STATE], axis=-1)
    xs = xs.reshape(b, s, SSM_HEADS, SSM_HEAD_DIM)
    bm = bm.reshape(b, s, SSM_GROUPS, SSM_STATE)
    cm = cm.reshape(b, s, SSM_GROUPS, SSM_STATE)
    dt = jax.nn.softplus(dt_raw.astype(jnp.float32).reshape(b, s, 2, SSM_HEADS)
                         + dt_bias.astype(jnp.float32))
    a_neg = -jnp.exp(a_log.astype(jnp.float32))
    y_fwd = ssd_scan(xs, dt[:, :, 0], a_neg[0], bm, cm)
    y_bwd = ssd_scan(xs[:, ::-1], dt[:, ::-1, 1], a_neg[1], bm[:, ::-1], cm[:, ::-1])[:, ::-1]
    y = (y_fwd + y_bwd + d_skip[:, None] * xs).reshape(b, s, SSM_INNER)
    return rmsnorm(y * jax.nn.silu(z), norm_g)


def peer_ffn(h, wq, keys, u_tab, v_tab):
    b, s, d = h.shape
    qry = (h @ wq).reshape(b, s, PEER_HEADS, 2, PEER_HALF)
    sc = jnp.einsum('bshcd,hckd->bshck', qry, keys).astype(jnp.float32)
    v1, i1 = lax.top_k(sc[..., 0, :], PEER_TOPK)
    v2, i2 = lax.top_k(sc[..., 1, :], PEER_TOPK)
    cand = (v1[..., :, None] + v2[..., None, :]).reshape(b, s, PEER_HEADS, PEER_TOPK * PEER_TOPK)
    cand_idx = (i1[..., :, None] * PEER_KEYS + i2[..., None, :]).reshape(b, s, PEER_HEADS, PEER_TOPK * PEER_TOPK)
    best, pos = lax.top_k(cand, PEER_TOPK)
    idx = jnp.take_along_axis(cand_idx, pos, axis=-1)
    gate = jax.nn.softmax(best, axis=-1)
    nb = s // PEER_BLOCK
    hk = PEER_HEADS * PEER_TOPK
    hb = h.reshape(b, nb, PEER_BLOCK, d).transpose(1, 0, 2, 3)
    ib = idx.reshape(b, nb, PEER_BLOCK, hk).transpose(1, 0, 2, 3)
    gb = gate.reshape(b, nb, PEER_BLOCK, hk).transpose(1, 0, 2, 3)

    def expert_block(args):
        hx, ix, gx = args
        u = jnp.take(u_tab, ix, axis=0)
        act = jax.nn.gelu(jnp.einsum('btkd,btd->btk', u, hx).astype(jnp.float32), approximate=False)
        wgt = (act * gx).astype(hx.dtype)
        return jnp.einsum('btk,btkd->btd', wgt, jnp.take(v_tab, ix, axis=0))

    out = lax.map(expert_block, (hb, ib, gb))
    return out.transpose(1, 0, 2, 3).reshape(b, s, d)


def setup_inputs(seed: int = 0) -> dict:
    key = jax.random.key(seed)
    ks = jax.random.split(key, 24)
    f32 = jnp.float32

    def nrm(k, shape, scale):
        return jax.random.normal(k, shape, f32) * scale

    def gain(k, shape):
        return 1.0 + 0.02 * jax.random.normal(k, shape, f32)

    x = nrm(ks[0], (BATCH, SEQ, D_MODEL), 1.0)
    offsets = jax.random.randint(ks[1], (BATCH, 1), 0, 4096, dtype=jnp.int32)
    positions = offsets + jnp.arange(SEQ, dtype=jnp.int32)[None, :]
    norm1_g = gain(ks[2], (DEPTH, D_MODEL))
    w_in = nrm(ks[3], (DEPTH, D_MODEL, IN_COLS), D_MODEL ** -0.5)
    gate_b = nrm(ks[4], (DEPTH, GATE_COLS), 0.01)
    diff_lam = nrm(ks[5], (DEPTH, 4, ATT_HEAD_DIM), 0.1)
    subln_g = gain(ks[6], (DEPTH, ATT_V_DIM))
    w_att_br = nrm(ks[7], (DEPTH, V_COLS, D_MODEL), V_COLS ** -0.5)
    conv_w = nrm(ks[8], (DEPTH, SSM_CONV, XBC_COLS), SSM_CONV ** -0.5)
    conv_b = nrm(ks[9], (DEPTH, XBC_COLS), 0.01)
    u = jax.random.uniform(ks[10], (DEPTH, 2, SSM_HEADS), f32)
    dt0 = jnp.exp(u * (math.log(DT_MAX) - math.log(DT_MIN)) + math.log(DT_MIN))
    dt_bias = dt0 + jnp.log(-jnp.expm1(-dt0))
    a_log = jnp.log(jax.random.uniform(ks[11], (DEPTH, 2, SSM_HEADS), f32, 1.0, 16.0))
    d_skip = gain(ks[12], (DEPTH, SSM_HEADS))
    ssm_norm_g = gain(ks[13], (DEPTH, SSM_INNER))
    w_ssm_br = nrm(ks[14], (DEPTH, SSM_INNER, D_MODEL), SSM_INNER ** -0.5)
    w_out = nrm(ks[15], (DEPTH, D_MODEL, D_MODEL), D_MODEL ** -0.5)
    norm2_g = gain(ks[16], (DEPTH, D_MODEL))
    peer_wq = nrm(ks[17], (DEPTH, D_MODEL, PEER_HEADS * PEER_KEY_DIM), D_MODEL ** -0.5)
    peer_keys = nrm(ks[18], (DEPTH, PEER_HEADS, 2, PEER_KEYS, PEER_HALF), PEER_HALF ** -0.5)
    peer_u = nrm(ks[19], (DEPTH, PEER_EXPERTS, D_MODEL), D_MODEL ** -0.5)
    peer_v = nrm(ks[20], (DEPTH, PEER_EXPERTS, D_MODEL), PEER_TOPK ** -0.5)
    final_g = gain(ks[21], (D_MODEL,))
    return {"x": x, "positions": positions, "norm1_g": norm1_g, "w_in": w_in,
            "gate_b": gate_b, "diff_lam": diff_lam, "subln_g": subln_g,
            "w_att_br": w_att_br, "conv_w": conv_w, "conv_b": conv_b,
            "dt_bias": dt_bias, "a_log": a_log, "d_skip": d_skip,
            "ssm_norm_g": ssm_norm_g, "w_ssm_br": w_ssm_br, "w_out": w_out,
            "norm2_g": norm2_g, "peer_wq": peer_wq, "peer_keys": peer_keys,
            "peer_u": peer_u, "peer_v": peer_v, "final_g": final_g}


def reference(x, positions, norm1_g, w_in, gate_b, diff_lam, subln_g, w_att_br,
              conv_w, conv_b, dt_bias, a_log, d_skip, ssm_norm_g, w_ssm_br, w_out,
              norm2_g, peer_wq, peer_keys, peer_u, peer_v, final_g):
    b, s, _ = x.shape
    cos, sin = rotary_tables(positions, x.dtype)
    for l in range(DEPTH):
        lam_init = 0.8 - 0.6 * math.exp(-0.3 * l)
        h = rmsnorm(x, norm1_g[l])
        proj = h @ w_in[l]
        q, k, v, z, xbc, dt_raw, gate_logits = jnp.split(proj, SPLIT_POINTS, axis=-1)
        q = partial_rotary(q.reshape(b, s, ATT_HEADS, 2, ATT_HEAD_DIM), cos, sin)
        k = partial_rotary(k.reshape(b, s, ATT_HEADS, 2, ATT_HEAD_DIM), cos, sin)
        v = v.reshape(b, s, ATT_HEADS, ATT_V_DIM)
        lp = diff_lam[l].astype(jnp.float32)
        lam = jnp.exp(jnp.sum(lp[0] * lp[1])) - jnp.exp(jnp.sum(lp[2] * lp[3])) + lam_init
        att = rmsnorm(diff_attention(q, k, v, lam), subln_g[l]) * (1.0 - lam_init)
        a_out = att.reshape(b, s, V_COLS) @ w_att_br[l]
        m = mamba2_bidir(z, xbc, dt_raw, conv_w[l], conv_b[l], dt_bias[l], a_log[l],
                         d_skip[l], ssm_norm_g[l])
        m_out = m @ w_ssm_br[l]
        g_att, g_ssm = jnp.split(jax.nn.sigmoid(gate_logits + gate_b[l]), 2, axis=-1)
        x = x + (g_att * a_out + g_ssm * m_out) @ w_out[l]
        x = x + peer_ffn(rmsnorm(x, norm2_g[l]), peer_wq[l], peer_keys[l], peer_u[l], peer_v[l])
    return rmsnorm(x, final_g)
```

```python
import functools
import math

import jax
import jax.numpy as jnp
from jax import lax
from jax.experimental import pallas as pl
from jax.experimental.pallas import tpu as pltpu

F32 = jnp.float32
BF16 = jnp.bfloat16

LANES = 128
VMEM_LIMIT = 48 * 1024 * 1024

ATT_HEADS = 8
ATT_HEAD_DIM = 64
ROPE_DIM = 16
ROPE_THETA = 500000.0
SSM_HEADS = 32
SSM_HEAD_DIM = 64
SSM_GROUPS = 8
SSM_STATE = 128
SSM_CONV = 5
SSM_CHUNK = 128
PEER_HEADS = 8
PEER_KEYS = 128
PEER_TOPK = 16
RMS_EPS = 1e-6
NEG_INF = float("-inf")
LOG2E = 1.4426950408889634


def _params(*sem):
    return pltpu.CompilerParams(dimension_semantics=sem, vmem_limit_bytes=VMEM_LIMIT)


def _rmsnorm_body(x_ref, g_ref, o_ref):
    x = x_ref[...].astype(F32)
    ms = jnp.mean(x * x, axis=-1, keepdims=True)
    o_ref[...] = (x * lax.rsqrt(ms + RMS_EPS) * g_ref[...]).astype(o_ref.dtype)


def rmsnorm(x, g, out_dtype=BF16, tm=512):
    n, d = x.shape
    tm = min(tm, n)
    return pl.pallas_call(
        _rmsnorm_body,
        grid=(n // tm,),
        in_specs=[pl.BlockSpec((tm, d), lambda i: (i, 0)),
                  pl.BlockSpec((1, d), lambda i: (0, 0))],
        out_specs=pl.BlockSpec((tm, d), lambda i: (i, 0)),
        out_shape=jax.ShapeDtypeStruct((n, d), out_dtype),
        compiler_params=_params("parallel"),
        name="rmsnorm",
    )(x, g.reshape(1, d).astype(F32))


def _mm_body(a_ref, w_ref, o_ref):
    o_ref[...] = jnp.dot(a_ref[...], w_ref[...], preferred_element_type=F32).astype(o_ref.dtype)


def _mm_res_body(a_ref, w_ref, r_ref, o_ref):
    acc = jnp.dot(a_ref[...], w_ref[...], preferred_element_type=F32)
    o_ref[...] = (r_ref[...].astype(F32) + acc).astype(o_ref.dtype)


def matmul(a, w, out_dtype, residual=None, tm=512, tn=1024):
    m, k = a.shape
    n = w.shape[1]
    tm = min(tm, m)
    tn = min(tn, n)
    assert m % tm == 0 and n % tn == 0
    in_specs = [pl.BlockSpec((tm, k), lambda j, i: (i, 0)),
                pl.BlockSpec((k, tn), lambda j, i: (0, j))]
    args = [a, w]
    body = _mm_body
    if residual is not None:
        in_specs.append(pl.BlockSpec((tm, tn), lambda j, i: (i, j)))
        args.append(residual)
        body = _mm_res_body
    return pl.pallas_call(
        body,
        grid=(n // tn, m // tm),
        in_specs=in_specs,
        out_specs=pl.BlockSpec((tm, tn), lambda j, i: (i, j)),
        out_shape=jax.ShapeDtypeStruct((m, n), out_dtype),
        compiler_params=_params("parallel", "parallel"),
        name="matmul",
    )(*args)


def _inproj_body(a_ref, w_ref, cos_ref, sa_ref, sb_ref, o_ref, *, q_scale):
    j = pl.program_id(0)
    acc = jnp.dot(a_ref[...], w_ref[...], preferred_element_type=F32)

    @pl.when(j >= 2)
    def _():
        o_ref[...] = acc.astype(o_ref.dtype)

    @pl.when(j < 2)
    def _():
        scale = jnp.where(j == 0, q_scale, 1.0).astype(F32)
        cosf = cos_ref[...] * scale
        sa = sa_ref[...] * scale
        sb = sb_ref[...] * scale
        tn = acc.shape[1]
        for g in range(tn // LANES):
            t = acc[:, g * LANES:(g + 1) * LANES]
            half = ROPE_DIM // 2
            r = (t * cosf + pltpu.roll(t, half, axis=1) * sa
                 + pltpu.roll(t, LANES - half, axis=1) * sb)
            o_ref[:, g * LANES:(g + 1) * LANES] = r.astype(o_ref.dtype)


def inproj_rotary(h, w, cosf, sin_a, sin_b, q_scale, tm=512, tn=1024):
    m, k = h.shape
    n = w.shape[1]
    tm = min(tm, m)
    assert m % tm == 0 and n % tn == 0
    tab = pl.BlockSpec((tm, LANES), lambda j, i: (i, 0))
    return pl.pallas_call(
        functools.partial(_inproj_body, q_scale=q_scale),
        grid=(n // tn, m // tm),
        in_specs=[pl.BlockSpec((tm, k), lambda j, i: (i, 0)),
                  pl.BlockSpec((k, tn), lambda j, i: (0, j)),
                  tab, tab, tab],
        out_specs=pl.BlockSpec((tm, tn), lambda j, i: (i, j)),
        out_shape=jax.ShapeDtypeStruct((m, n), BF16),
        compiler_params=_params("parallel", "parallel"),
        name="inproj_rotary",
    )(h, w, cosf, sin_a, sin_b)


def rotary_tables(positions):
    half = ROPE_DIM // 2
    inv_freq = ROPE_THETA ** (-jnp.arange(0, ROPE_DIM, 2, dtype=F32) / ROPE_DIM)
    ang = positions.reshape(-1).astype(F32)[:, None] * inv_freq
    cos, sin = jnp.cos(ang), jnp.sin(ang)
    n = ang.shape[0]
    one = jnp.ones((n, ATT_HEAD_DIM - ROPE_DIM), F32)
    zero8 = jnp.zeros((n, half), F32)
    zero = jnp.zeros((n, ATT_HEAD_DIM - ROPE_DIM), F32)
    cos64 = jnp.concatenate([cos, cos, one], axis=1)
    sa64 = jnp.concatenate([zero8, sin, zero], axis=1)
    sb64 = jnp.concatenate([-sin, zero8, zero], axis=1)
    dup = lambda t: jnp.concatenate([t, t], axis=1)
    return dup(cos64), dup(sa64), dup(sb64)


def _attn_body(lam_ref, g_ref, q_ref, k_ref, v_ref, o_ref, m_ref, l_ref, acc_ref,
               *, tk, lam_init):
    tq = q_ref.shape[0]
    s_len = k_ref.shape[0]
    lane = lax.broadcasted_iota(jnp.int32, (1, LANES), 1)
    first = lane < ATT_HEAD_DIM
    q = q_ref[...]
    zero = jnp.zeros_like(q)
    qs = (jnp.where(first, q, zero), jnp.where(first, zero, q))

    m_ref[...] = jnp.full(m_ref.shape, NEG_INF, F32)
    l_ref[...] = jnp.zeros(l_ref.shape, F32)
    acc_ref[...] = jnp.zeros(acc_ref.shape, F32)

    def step(kc, carry):
        off = pl.multiple_of(kc * tk, tk)
        k = k_ref[pl.ds(off, tk), :]
        v = v_ref[pl.ds(off, tk), :]
        for c in range(2):
            s = lax.dot_general(qs[c], k, (((1,), (1,)), ((), ())),
                                preferred_element_type=F32)
            m_old = m_ref[c]
            m_new = jnp.maximum(m_old, jnp.max(s, axis=-1, keepdims=True))
            alpha = jnp.exp2(m_old - m_new)
            p = jnp.exp2(s - m_new)
            l_ref[c] = alpha * l_ref[c] + jnp.sum(p, axis=-1, keepdims=True)
            acc_ref[c] = alpha * acc_ref[c] + jnp.dot(p.astype(BF16), v,
                                                      preferred_element_type=F32)
            m_ref[c] = m_new
        return carry

    lax.fori_loop(0, s_len // tk, step, 0)

    lp = lam_ref[...].astype(F32)
    lam = (jnp.exp(jnp.sum(lp[0:1] * lp[1:2], axis=-1, keepdims=True))
           - jnp.exp(jnp.sum(lp[2:3] * lp[3:4], axis=-1, keepdims=True)) + lam_init)
    out = acc_ref[0] / l_ref[0] - lam * (acc_ref[1] / l_ref[1])
    ms = jnp.mean(out * out, axis=-1, keepdims=True)
    o_ref[...] = (out * lax.rsqrt(ms + RMS_EPS) * g_ref[...] * (1.0 - lam_init)).astype(o_ref.dtype)


def diff_attention(proj, v_blk, lam_params, subln_g, lam_init, batch, seq, tq=512, tk=512):
    n = batch * seq
    tq = min(tq, seq)
    tk = min(tk, seq)
    nq = seq // tq
    h = ATT_HEADS
    return pl.pallas_call(
        functools.partial(_attn_body, tk=tk, lam_init=lam_init),
        grid=(batch, h, nq),
        in_specs=[pl.BlockSpec((4, ATT_HEAD_DIM), lambda b, hh, i: (0, 0)),
                  pl.BlockSpec((1, LANES), lambda b, hh, i: (0, 0)),
                  pl.BlockSpec((tq, LANES), lambda b, hh, i: (b * nq + i, hh)),
                  pl.BlockSpec((seq, LANES), lambda b, hh, i: (b, h + hh)),
                  pl.BlockSpec((seq, LANES), lambda b, hh, i: (b, v_blk + hh))],
        out_specs=pl.BlockSpec((tq, LANES), lambda b, hh, i: (b * nq + i, hh)),
        out_shape=jax.ShapeDtypeStruct((n, h * LANES), BF16),
        scratch_shapes=[pltpu.VMEM((2, tq, 1), F32), pltpu.VMEM((2, tq, 1), F32),
                        pltpu.VMEM((2, tq, LANES), F32)],
        compiler_params=_params("parallel", "parallel", "parallel"),
        name="diff_attention",
    )(lam_params.astype(F32), subln_g.reshape(1, LANES).astype(F32), proj, proj, proj)


HALO = 8


def _conv_body(prev_ref, cur_ref, next_ref, w_ref, b_ref, o_ref, *, n_seq_blocks):
    i = pl.program_id(1)
    ts = cur_ref.shape[0]
    prev = jnp.where(i > 0, prev_ref[...].astype(F32), 0.0)
    nxt = jnp.where(i < n_seq_blocks - 1, next_ref[...].astype(F32), 0.0)
    ext = jnp.concatenate([prev, cur_ref[...].astype(F32), nxt], axis=0)
    w = w_ref[...]
    acc = jnp.zeros(cur_ref.shape, F32) + b_ref[...]
    pad = SSM_CONV // 2
    for kk in range(SSM_CONV):
        start = HALO - pad + kk
        acc = acc + ext[start:start + ts, :] * w[kk:kk + 1, :]
    o_ref[...] = (acc * jax.nn.sigmoid(acc)).astype(o_ref.dtype)


def conv_silu(proj, col_off, conv_w, conv_b, batch, seq, ts=512, tc=512):
    n = batch * seq
    c = conv_w.shape[1]
    ts = min(ts, seq)
    nsb = seq // ts
    assert col_off % tc == 0 and c % tc == 0 and ts % HALO == 0
    cb = col_off // tc
    hb = ts // HALO
    last_halo = n // HALO - 1

    def prev_map(b, i, j):
        return (jnp.maximum((b * nsb + i) * hb - 1, 0), cb + j)

    def next_map(b, i, j):
        return (jnp.minimum((b * nsb + i + 1) * hb, last_halo), cb + j)

    return pl.pallas_call(
        functools.partial(_conv_body, n_seq_blocks=nsb),
        grid=(batch, nsb, c // tc),
        in_specs=[pl.BlockSpec((HALO, tc), prev_map),
                  pl.BlockSpec((ts, tc), lambda b, i, j: (b * nsb + i, cb + j)),
                  pl.BlockSpec((HALO, tc), next_map),
                  pl.BlockSpec((SSM_CONV, tc), lambda b, i, j: (0, j)),
                  pl.BlockSpec((1, tc), lambda b, i, j: (0, j))],
        out_specs=pl.BlockSpec((ts, tc), lambda b, i, j: (b * nsb + i, j)),
        out_shape=jax.ShapeDtypeStruct((n, c), BF16),
        compiler_params=_params("parallel", "parallel", "parallel"),
        name="conv_silu",
    )(proj, proj, proj, conv_w.astype(F32), conv_b.reshape(1, c).astype(F32))


def _softplus(x):
    return jnp.maximum(x, 0.0) + jnp.log1p(jnp.exp(-jnp.abs(x)))


def _ssd_body(x_ref, b_ref, c_ref, dt_ref, dtt_ref, bias_ref, biast_ref, alog_ref, alogt_ref,
              y_ref, state_ref, *, reverse):
    q = SSM_CHUNK
    hpg = SSM_HEADS // SSM_GROUPS
    gw = hpg * SSM_HEAD_DIM
    hi = lax.Precision.HIGHEST

    @pl.when(pl.program_id(1) == 0)
    def _():
        state_ref[...] = jnp.zeros(state_ref.shape, F32)

    row = lax.broadcasted_iota(jnp.int32, (q, q), 0)
    col = lax.broadcasted_iota(jnp.int32, (q, q), 1)
    keep = (col >= row) if reverse else (col <= row)
    incl = keep.astype(F32)
    incl_t = ((row >= col) if reverse else (row <= col)).astype(F32)

    dt = _softplus(dt_ref[...] + bias_ref[...])
    dtt = _softplus(dtt_ref[...] + biast_ref[...])
    a = dt * (-jnp.exp(alog_ref[...]))
    at = dtt * (-jnp.exp(alogt_ref[...]))
    cum = jnp.dot(incl, a, precision=hi, preferred_element_type=F32)
    cum_t = jnp.dot(at, incl_t, precision=hi, preferred_element_type=F32)
    total = jnp.sum(a, axis=0, keepdims=True)

    hid = lax.broadcasted_iota(jnp.int32, (SSM_HEADS, SSM_HEADS * SSM_HEAD_DIM), 0)
    lid = lax.broadcasted_iota(jnp.int32, (SSM_HEADS, SSM_HEADS * SSM_HEAD_DIM), 1)
    expand = (lid // SSM_HEAD_DIM == hid).astype(F32)

    def ex(v):
        return jnp.dot(v, expand, precision=hi, preferred_element_type=F32)

    xs = x_ref[...].astype(F32)
    xdt = xs * ex(dt)
    xdt_b = xdt.astype(BF16)
    xdec = (xdt * ex(jnp.exp(total - cum))).astype(BF16)
    dec_out = ex(jnp.exp(cum))
    dec_chunk = ex(jnp.exp(total))

    lane_g = lax.broadcasted_iota(jnp.int32, (1, gw), 1) // SSM_HEAD_DIM

    for g in range(SSM_GROUPS):
        bg = b_ref[:, g * SSM_STATE:(g + 1) * SSM_STATE]
        cg = c_ref[:, g * SSM_STATE:(g + 1) * SSM_STATE]
        cb = lax.dot_general(cg, bg, (((1,), (1,)), ((), ())), preferred_element_type=F32)
        xg = xdt_b[:, g * gw:(g + 1) * gw]
        ms, xbd = [], []
        for hh in range(hpg):
            head = g * hpg + hh
            diff = cum[:, head:head + 1] - cum_t[head:head + 1, :]
            decay = jnp.exp(jnp.where(keep, diff, -1e30))
            ms.append((cb * decay).astype(BF16))
            xbd.append(jnp.where(lane_g == hh, xg, jnp.zeros_like(xg)))
        m_cat = jnp.concatenate(ms, axis=1)
        x_bd = jnp.concatenate(xbd, axis=0)
        y_diag = jnp.dot(m_cat, x_bd, preferred_element_type=F32)

        st = state_ref[g]
        y_off = jnp.dot(cg, st.astype(BF16), preferred_element_type=F32) * dec_out[:, g * gw:(g + 1) * gw]
        y_ref[:, g * gw:(g + 1) * gw] = (y_diag + y_off).astype(y_ref.dtype)

        new = lax.dot_general(bg, xdec[:, g * gw:(g + 1) * gw], (((0,), (0,)), ((), ())),
                              preferred_element_type=F32)
        state_ref[g] = st * dec_chunk[:, g * gw:(g + 1) * gw] + new


def ssd_scan(xc, dt_raw, dt_bias, a_log, batch, seq, reverse):
    n = batch * seq
    q = SSM_CHUNK
    nc = seq // q
    hp = SSM_HEADS * SSM_HEAD_DIM
    gn = SSM_GROUPS * SSM_STATE
    assert xc.shape[1] == hp + 2 * gn and hp == 2 * gn

    def cidx(c):
        return nc - 1 - c if reverse else c

    return pl.pallas_call(
        functools.partial(_ssd_body, reverse=reverse),
        grid=(batch, nc),
        in_specs=[pl.BlockSpec((q, hp), lambda b, c: (b * nc + cidx(c), 0)),
                  pl.BlockSpec((q, gn), lambda b, c: (b * nc + cidx(c), 2)),
                  pl.BlockSpec((q, gn), lambda b, c: (b * nc + cidx(c), 3)),
                  pl.BlockSpec((q, SSM_HEADS), lambda b, c: (b * nc + cidx(c), 0)),
                  pl.BlockSpec((SSM_HEADS, q), lambda b, c: (0, b * nc + cidx(c))),
                  pl.BlockSpec((1, SSM_HEADS), lambda b, c: (0, 0)),
                  pl.BlockSpec((SSM_HEADS, 1), lambda b, c: (0, 0)),
                  pl.BlockSpec((1, SSM_HEADS), lambda b, c: (0, 0)),
                  pl.BlockSpec((SSM_HEADS, 1), lambda b, c: (0, 0))],
        out_specs=pl.BlockSpec((q, hp), lambda b, c: (b * nc + cidx(c), 0)),
        out_shape=jax.ShapeDtypeStruct((n, hp), BF16),
        scratch_shapes=[pltpu.VMEM((SSM_GROUPS, SSM_STATE, hp // SSM_GROUPS), F32)],
        compiler_params=_params("parallel", "arbitrary"),
        name="ssd_scan_bwd" if reverse else "ssd_scan_fwd",
    )(xc, xc, xc, dt_raw, dt_raw.T, dt_bias.reshape(1, -1).astype(F32),
      dt_bias.reshape(-1, 1).astype(F32), a_log.reshape(1, -1).astype(F32),
      a_log.reshape(-1, 1).astype(F32))


def _gated_norm_body(yf_ref, yb_ref, xs_ref, z_ref, d_ref, g_ref, o_ref):
    z = z_ref[...].astype(F32)
    y = (yf_ref[...].astype(F32) + yb_ref[...].astype(F32)
         + d_ref[...] * xs_ref[...].astype(F32)) * (z * jax.nn.sigmoid(z))
    ms = jnp.mean(y * y, axis=-1, keepdims=True)
    o_ref[...] = (y * lax.rsqrt(ms + RMS_EPS) * g_ref[...]).astype(o_ref.dtype)


def gated_norm(y_f, y_b, xc, proj, z_col_block, d_skip, norm_g, tm=256):
    n, c = y_f.shape
    tm = min(tm, n)
    row = lambda i: (i, 0)
    return pl.pallas_call(
        _gated_norm_body,
        grid=(n // tm,),
        in_specs=[pl.BlockSpec((tm, c), row), pl.BlockSpec((tm, c), row),
                  pl.BlockSpec((tm, c), row),
                  pl.BlockSpec((tm, c), lambda i: (i, z_col_block)),
                  pl.BlockSpec((1, c), lambda i: (0, 0)), pl.BlockSpec((1, c), lambda i: (0, 0))],
        out_specs=pl.BlockSpec((tm, c), row),
        out_shape=jax.ShapeDtypeStruct((n, c), BF16),
        compiler_params=_params("parallel"),
        name="gated_norm",
    )(y_f, y_b, xc, proj, jnp.repeat(d_skip.astype(F32), SSM_HEAD_DIM).reshape(1, c),
      norm_g.reshape(1, c).astype(F32))


def _merge_body(att_ref, m_ref, wa_ref, ws_ref, ga_ref, gs_ref, ba_ref, bs_ref, o_ref):
    a_out = jnp.dot(att_ref[...], wa_ref[...], preferred_element_type=F32)
    m_out = jnp.dot(m_ref[...], ws_ref[...], preferred_element_type=F32)
    g_att = jax.nn.sigmoid(ga_ref[...] + ba_ref[...])
    g_ssm = jax.nn.sigmoid(gs_ref[...] + bs_ref[...])
    o_ref[...] = (g_att * a_out + g_ssm * m_out).astype(o_ref.dtype)


def branch_merge(att, m, w_att, w_ssm, gate_logits, gate_b, tm=512, tn=512):
    n, d = att.shape[0], w_att.shape[1]
    tm = min(tm, n)
    nb = d // tn
    return pl.pallas_call(
        _merge_body,
        grid=(nb, n // tm),
        in_specs=[pl.BlockSpec((tm, att.shape[1]), lambda j, i: (i, 0)),
                  pl.BlockSpec((tm, m.shape[1]), lambda j, i: (i, 0)),
                  pl.BlockSpec((w_att.shape[0], tn), lambda j, i: (0, j)),
                  pl.BlockSpec((w_ssm.shape[0], tn), lambda j, i: (0, j)),
                  pl.BlockSpec((tm, tn), lambda j, i: (i, j)),
                  pl.BlockSpec((tm, tn), lambda j, i: (i, nb + j)),
                  pl.BlockSpec((1, tn), lambda j, i: (0, j)),
                  pl.BlockSpec((1, tn), lambda j, i: (0, nb + j))],
        out_specs=pl.BlockSpec((tm, tn), lambda j, i: (i, j)),
        out_shape=jax.ShapeDtypeStruct((n, d), BF16),
        compiler_params=_params("parallel", "parallel"),
        name="branch_merge",
    )(att, m, w_att, w_ssm, gate_logits, gate_logits, gate_b.reshape(1, -1).astype(F32),
      gate_b.reshape(1, -1).astype(F32))


def _peer_score_body(h_ref, wq_ref, keys_ref, o_ref):
    qry = jnp.dot(h_ref[...], wq_ref[...], preferred_element_type=F32).astype(BF16)
    for hc in range(2 * PEER_HEADS):
        qh = qry[:, hc * LANES:(hc + 1) * LANES]
        o_ref[hc] = lax.dot_general(keys_ref[hc], qh, (((1,), (1,)), ((), ())),
                                    preferred_element_type=F32)


def peer_scores(h, wq, keys, tm=512):
    n, d = h.shape
    tm = min(tm, n)
    nhc = 2 * PEER_HEADS
    return pl.pallas_call(
        _peer_score_body,
        grid=(n // tm,),
        in_specs=[pl.BlockSpec((tm, d), lambda i: (i, 0)),
                  pl.BlockSpec(wq.shape, lambda i: (0, 0)),
                  pl.BlockSpec(keys.shape, lambda i: (0, 0, 0))],
        out_specs=pl.BlockSpec((nhc, PEER_KEYS, tm), lambda i: (0, 0, i)),
        out_shape=jax.ShapeDtypeStruct((nhc, PEER_KEYS, n), F32),
        compiler_params=_params("parallel"),
        name="peer_scores",
    )(h, wq, keys)


def _top16_rows(x):
    r = x.shape[0]
    rid = lax.broadcasted_iota(jnp.int32, x.shape, 0)
    out = []
    for _ in range(PEER_TOPK):
        m = jnp.max(x, axis=0, keepdims=True)
        first = jnp.min(jnp.where(x == m, rid, r), axis=0, keepdims=True)
        x = jnp.where(rid == first, NEG_INF, x)
        out.append(m)
    return out


def _peer_route_body(sc_ref, st_ref):
    t = sc_ref.shape[2]
    sub = lax.broadcasted_iota(jnp.int32, (8, t), 0)
    for h in range(PEER_HEADS):
        va = _top16_rows(sc_ref[2 * h])
        vb = _top16_rows(sc_ref[2 * h + 1])

        def stack8(rows):
            acc = jnp.zeros((8, t), F32)
            for i, rw in enumerate(rows):
                acc = jnp.where(sub == i, rw, acc)
            return acc

        va_lo, va_hi = stack8(va[:8]), stack8(va[8:])
        vb_hi = stack8(vb[8:])
        groups = [va_lo + vb[0], va_hi + vb[0], va_lo + vb[1]]
        for qq, lim in ((2, 5), (3, 4), (4, 3), (5, 2), (6, 2), (7, 2)):
            groups.append(jnp.where(sub < lim, va_lo + vb[qq], NEG_INF))
        groups.append(va[0] + vb_hi)
        cand = jnp.concatenate(groups, axis=0)
        best = _top16_rows(cand)
        z = jnp.zeros((1, t), F32)
        for bv in best:
            z = z + jnp.exp(bv - best[0])
        st_ref[0, h:h + 1, :] = best[PEER_TOPK - 1]
        st_ref[1, h:h + 1, :] = va[0]
        st_ref[2, h:h + 1, :] = vb[0]
        st_ref[3, h:h + 1, :] = z


def peer_route(sc_t, tt=256):
    nhc, kk, n = sc_t.shape
    tt = min(tt, n)
    return pl.pallas_call(
        _peer_route_body,
        grid=(n // tt,),
        in_specs=[pl.BlockSpec((nhc, kk, tt), lambda i: (0, 0, i))],
        out_specs=pl.BlockSpec((4, PEER_HEADS, tt), lambda i: (0, 0, i)),
        out_shape=jax.ShapeDtypeStruct((4, PEER_HEADS, n), F32),
        compiler_params=_params("parallel"),
        name="peer_route",
    )(sc_t)


def _erf(x):
    return lax.erf(x)


def _peer_mix_body(h_ref, u_ref, vt_ref, sc_ref, st_ref, x_ref, o_ref, acc_ref, ea_ref, eb_ref):
    c = pl.program_id(1)
    ec = u_ref.shape[0]
    tm = h_ref.shape[0]
    ipc = ec // PEER_KEYS

    @pl.when(c == 0)
    def _():
        acc_ref[...] = jnp.zeros(acc_ref.shape, F32)
        for h in range(PEER_HEADS):
            inv_z = 1.0 / st_ref[3, h:h + 1, :]
            ea_ref[h] = jnp.exp(sc_ref[2 * h] - st_ref[1, h:h + 1, :]) * inv_z
            eb_ref[h] = jnp.exp(sc_ref[2 * h + 1] - st_ref[2, h:h + 1, :])

    s_t = lax.dot_general(u_ref[...], h_ref[...], (((1,), (1,)), ((), ())),
                          preferred_element_type=F32)
    w_parts = []
    for ii in range(ipc):
        i = c * ipc + ii
        gate = jnp.zeros((PEER_KEYS, tm), F32)
        for h in range(PEER_HEADS):
            a_row = sc_ref[2 * h, pl.ds(i, 1), :]
            ea_row = ea_ref[h, pl.ds(i, 1), :]
            tau = st_ref[0, h:h + 1, :]
            ssum = a_row + sc_ref[2 * h + 1]
            gate = gate + jnp.where(ssum >= tau, ea_row * eb_ref[h], 0.0)
        s_blk = s_t[ii * PEER_KEYS:(ii + 1) * PEER_KEYS, :]
        act = 0.5 * s_blk * (1.0 + _erf(s_blk * (2.0 ** -0.5)))
        w_parts.append((act * gate).astype(BF16))
    w_t = jnp.concatenate(w_parts, axis=0) if ipc > 1 else w_parts[0]
    acc_ref[...] += jnp.dot(vt_ref[...], w_t, preferred_element_type=F32)

    @pl.when(c == pl.num_programs(1) - 1)
    def _():
        o_ref[...] = x_ref[...] + acc_ref[...].T


def peer_mix(h, u_tab, vt_tab, sc_t, stats, x_res, tm=256, ec=512):
    n, d = h.shape
    e = u_tab.shape[0]
    tm = min(tm, n)
    nhc = 2 * PEER_HEADS
    return pl.pallas_call(
        _peer_mix_body,
        grid=(n // tm, e // ec),
        in_specs=[pl.BlockSpec((tm, d), lambda t, c: (t, 0)),
                  pl.BlockSpec((ec, d), lambda t, c: (c, 0)),
                  pl.BlockSpec((d, ec), lambda t, c: (0, c)),
                  pl.BlockSpec((nhc, PEER_KEYS, tm), lambda t, c: (0, 0, t)),
                  pl.BlockSpec((4, PEER_HEADS, tm), lambda t, c: (0, 0, t)),
                  pl.BlockSpec((tm, d), lambda t, c: (t, 0))],
        out_specs=pl.BlockSpec((tm, d), lambda t, c: (t, 0)),
        out_shape=jax.ShapeDtypeStruct((n, d), F32),
        scratch_shapes=[pltpu.VMEM((d, tm), F32),
                        pltpu.VMEM((PEER_HEADS, PEER_KEYS, tm), F32),
                        pltpu.VMEM((PEER_HEADS, PEER_KEYS, tm), F32)],
        compiler_params=_params("parallel", "arbitrary"),
        name="peer_mix",
    )(h, u_tab, vt_tab, sc_t, stats, x_res)


def kernel(x, positions, norm1_g, w_in, gate_b, diff_lam, subln_g, w_att_br, conv_w, conv_b,
           dt_bias, a_log, d_skip, ssm_norm_g, w_ssm_br, w_out, norm2_g, peer_wq, peer_keys,
           peer_u, peer_v, final_g):
    batch, seq, d = x.shape
    depth = w_in.shape[0]
    n = batch * seq
    qk_cols = 2 * ATT_HEADS * 2 * ATT_HEAD_DIM
    qkv_cols = 3 * ATT_HEADS * 2 * ATT_HEAD_DIM
    z_cols = SSM_HEADS * SSM_HEAD_DIM
    xbc_cols = z_cols + 2 * SSM_GROUPS * SSM_STATE
    main_cols = qkv_cols + z_cols + xbc_cols
    dt_cols = 2 * SSM_HEADS

    cosf, sin_a, sin_b = rotary_tables(positions)
    q_scale = ATT_HEAD_DIM ** -0.5 * LOG2E
    xf = x.reshape(n, d)

    for l in range(depth):
        lam_init = 0.8 - 0.6 * math.exp(-0.3 * l)
        w_l = w_in[l]
        w_main = jnp.concatenate([w_l[:, :qk_cols], w_l[:, qkv_cols:main_cols],
                                  w_l[:, qk_cols:qkv_cols]], axis=1).astype(BF16)
        w_dt = w_l[:, main_cols:main_cols + dt_cols].astype(BF16)
        w_gate = w_l[:, main_cols + dt_cols:].astype(BF16)

        h1 = rmsnorm(xf, norm1_g[l])
        proj = inproj_rotary(h1, w_main, cosf, sin_a, sin_b, q_scale)
        dt_raw = matmul(h1, w_dt, F32)
        gate_logits = matmul(h1, w_gate, F32)

        att = diff_attention(proj, (qk_cols + z_cols + xbc_cols) // LANES, diff_lam[l],
                             subln_g[l], lam_init, batch, seq)

        xc = conv_silu(proj, qk_cols + z_cols, conv_w[l], conv_b[l], batch, seq)
        y_f = ssd_scan(xc, dt_raw[:, :SSM_HEADS], dt_bias[l, 0], a_log[l, 0], batch, seq, False)
        y_b = ssd_scan(xc, dt_raw[:, SSM_HEADS:], dt_bias[l, 1], a_log[l, 1], batch, seq, True)
        m = gated_norm(y_f, y_b, xc, proj, qk_cols // z_cols, d_skip[l], ssm_norm_g[l])

        merged = branch_merge(att, m, w_att_br[l].astype(BF16), w_ssm_br[l].astype(BF16),
                              gate_logits, gate_b[l])
        xf = matmul(merged, w_out[l].astype(BF16), F32, residual=xf)

        h2 = rmsnorm(xf, norm2_g[l])
        keys = peer_keys[l].reshape(2 * PEER_HEADS, PEER_KEYS, -1).astype(BF16)
        sc_t = peer_scores(h2, peer_wq[l].astype(BF16), keys)
        stats = peer_route(sc_t)
        xf = peer_mix(h2, peer_u[l].astype(BF16), peer_v[l].astype(BF16).T, sc_t, stats, xf)

    return rmsnorm(xf, final_g, out_dtype=x.dtype).reshape(batch, seq, d)
```

```python
import functools
import math

import jax
import jax.numpy as jnp
from jax import lax
from jax.experimental import pallas as pl
from jax.experimental.pallas import tpu as pltpu

F32 = jnp.float32
BF16 = jnp.bfloat16

LANES = 128
VMEM_LIMIT = 48 * 1024 * 1024

ATT_HEADS = 8
ATT_HEAD_DIM = 64
ROPE_DIM = 16
ROPE_THETA = 500000.0
SSM_HEADS = 32
SSM_HEAD_DIM = 64
SSM_GROUPS = 8
SSM_STATE = 128
SSM_CONV = 5
SSM_CHUNK = 128
PEER_HEADS = 8
PEER_KEYS = 128
PEER_TOPK = 16
RMS_EPS = 1e-6
NEG_INF = float("-inf")
LOG2E = 1.4426950408889634


def _params(*sem):
    return pltpu.CompilerParams(dimension_semantics=sem, vmem_limit_bytes=VMEM_LIMIT)


def _rmsnorm_body(x_ref, g_ref, o_ref):
    x = x_ref[...].astype(F32)
    ms = jnp.mean(x * x, axis=-1, keepdims=True)
    o_ref[...] = (x * lax.rsqrt(ms + RMS_EPS) * g_ref[...]).astype(o_ref.dtype)


def rmsnorm(x, g, out_dtype=BF16, tm=512):
    n, d = x.shape
    tm = min(tm, n)
    return pl.pallas_call(
        _rmsnorm_body,
        grid=(n // tm,),
        in_specs=[pl.BlockSpec((tm, d), lambda i: (i, 0)),
                  pl.BlockSpec((1, d), lambda i: (0, 0))],
        out_specs=pl.BlockSpec((tm, d), lambda i: (i, 0)),
        out_shape=jax.ShapeDtypeStruct((n, d), out_dtype),
        compiler_params=_params("parallel"),
        name="rmsnorm",
    )(x, g.reshape(1, d).astype(F32))


def _mm_body(a_ref, w_ref, o_ref):
    o_ref[...] = jnp.dot(a_ref[...], w_ref[...], preferred_element_type=F32).astype(o_ref.dtype)


def _mm_res_body(a_ref, w_ref, r_ref, o_ref):
    acc = jnp.dot(a_ref[...], w_ref[...], preferred_element_type=F32)
    o_ref[...] = (r_ref[...].astype(F32) + acc).astype(o_ref.dtype)


def matmul(a, w, out_dtype, residual=None, tm=512, tn=1024):
    m, k = a.shape
    n = w.shape[1]
    tm = min(tm, m)
    tn = min(tn, n)
    assert m % tm == 0 and n % tn == 0
    in_specs = [pl.BlockSpec((tm, k), lambda j, i: (i, 0)),
                pl.BlockSpec((k, tn), lambda j, i: (0, j))]
    args = [a, w]
    body = _mm_body
    if residual is not None:
        in_specs.append(pl.BlockSpec((tm, tn), lambda j, i: (i, j)))
        args.append(residual)
        body = _mm_res_body
    return pl.pallas_call(
        body,
        grid=(n // tn, m // tm),
        in_specs=in_specs,
        out_specs=pl.BlockSpec((tm, tn), lambda j, i: (i, j)),
        out_shape=jax.ShapeDtypeStruct((m, n), out_dtype),
        compiler_params=_params("parallel", "parallel"),
        name="matmul",
    )(*args)


def _inproj_body(a_ref, w_ref, cos_ref, sa_ref, sb_ref, o_ref, *, q_scale):
    j = pl.program_id(0)
    acc = jnp.dot(a_ref[...], w_ref[...], preferred_element_type=F32)

    @pl.when(j >= 2)
    def _():
        o_ref[...] = acc.astype(o_ref.dtype)

    @pl.when(j < 2)
    def _():
        scale = jnp.where(j == 0, q_scale, 1.0).astype(F32)
        cosf = cos_ref[...] * scale
        sa = sa_ref[...] * scale
        sb = sb_ref[...] * scale
        tn = acc.shape[1]
        for g in range(tn // LANES):
            t = acc[:, g * LANES:(g + 1) * LANES]
            half = ROPE_DIM // 2
            r = (t * cosf + pltpu.roll(t, half, axis=1) * sa
                 + pltpu.roll(t, LANES - half, axis=1) * sb)
            o_ref[:, g * LANES:(g + 1) * LANES] = r.astype(o_ref.dtype)


def inproj_rotary(h, w, cosf, sin_a, sin_b, q_scale, tm=512, tn=1024):
    m, k = h.shape
    n = w.shape[1]
    tm = min(tm, m)
    assert m % tm == 0 and n % tn == 0
    tab = pl.BlockSpec((tm, LANES), lambda j, i: (i, 0))
    return pl.pallas_call(
        functools.partial(_inproj_body, q_scale=q_scale),
        grid=(n // tn, m // tm),
        in_specs=[pl.BlockSpec((tm, k), lambda j, i: (i, 0)),
                  pl.BlockSpec((k, tn), lambda j, i: (0, j)),
                  tab, tab, tab],
        out_specs=pl.BlockSpec((tm, tn), lambda j, i: (i, j)),
        out_shape=jax.ShapeDtypeStruct((m, n), BF16),
        compiler_params=_params("parallel", "parallel"),
        name="inproj_rotary",
    )(h, w, cosf, sin_a, sin_b)


def rotary_tables(positions):
    half = ROPE_DIM // 2
    inv_freq = ROPE_THETA ** (-jnp.arange(0, ROPE_DIM, 2, dtype=F32) / ROPE_DIM)
    ang = positions.reshape(-1).astype(F32)[:, None] * inv_freq
    cos, sin = jnp.cos(ang), jnp.sin(ang)
    n = ang.shape[0]
    one = jnp.ones((n, ATT_HEAD_DIM - ROPE_DIM), F32)
    zero8 = jnp.zeros((n, half), F32)
    zero = jnp.zeros((n, ATT_HEAD_DIM - ROPE_DIM), F32)
    cos64 = jnp.concatenate([cos, cos, one], axis=1)
    sa64 = jnp.concatenate([zero8, sin, zero], axis=1)
    sb64 = jnp.concatenate([-sin, zero8, zero], axis=1)
    dup = lambda t: jnp.concatenate([t, t], axis=1)
    return dup(cos64), dup(sa64), dup(sb64)


def _attn_body(lam_ref, g_ref, q_ref, k_ref, v_ref, o_ref, qs_ref, m_ref, acc_ref, s_ref,
               *, tk, lam_init):
    tq = q_ref.shape[0]
    s_len = k_ref.shape[0]
    lane = lax.broadcasted_iota(jnp.int32, (1, LANES), 1)
    first = lane < ATT_HEAD_DIM
    q = q_ref[...]
    zero = jnp.zeros_like(q)
    qs_ref[0:tq, :] = jnp.where(first, q, zero)
    qs_ref[tq:2 * tq, :] = jnp.where(first, zero, q)
    m_ref[...] = jnp.full(m_ref.shape, NEG_INF, F32)
    acc_ref[...] = jnp.zeros(acc_ref.shape, F32)
    ones = jnp.ones((tk, LANES), BF16)

    def scores(kc, slot):
        off = pl.multiple_of(kc * tk, tk)
        s_ref[slot] = lax.dot_general(qs_ref[...], k_ref[pl.ds(off, tk), :],
                                      (((1,), (1,)), ((), ())),
                                      preferred_element_type=F32)

    def update(kc, slot):
        off = pl.multiple_of(kc * tk, tk)
        v_ext = jnp.concatenate([v_ref[pl.ds(off, tk), :], ones], axis=1)
        s = s_ref[slot]
        m_old = m_ref[...]
        m_new = jnp.maximum(m_old, jnp.max(s, axis=-1, keepdims=True))
        alpha = jnp.exp2(m_old - m_new)
        p = jnp.concatenate(
            [jnp.exp2(s[:, j * LANES:(j + 1) * LANES] - m_new) for j in range(tk // LANES)], axis=1)
        pv = jnp.dot(p.astype(BF16), v_ext, preferred_element_type=F32)
        acc_ref[...] = jnp.concatenate([alpha, alpha], axis=1) * acc_ref[...] + pv
        m_ref[...] = m_new

    n_chunks = s_len // tk
    assert n_chunks % 2 == 0
    scores(0, 0)

    def step(i, carry):
        scores(2 * i + 1, 1)
        update(2 * i, 0)
        scores(jnp.minimum(2 * i + 2, n_chunks - 1), 0)
        update(2 * i + 1, 1)
        return carry

    lax.fori_loop(0, n_chunks // 2, step, 0)

    lp = lam_ref[...].astype(F32)
    lam = (jnp.exp(jnp.sum(lp[0:1] * lp[1:2], axis=-1, keepdims=True))
           - jnp.exp(jnp.sum(lp[2:3] * lp[3:4], axis=-1, keepdims=True)) + lam_init)
    a1 = acc_ref[0:tq, :]
    a2 = acc_ref[tq:2 * tq, :]
    out = a1[:, :LANES] / a1[:, LANES:] - lam * (a2[:, :LANES] / a2[:, LANES:])
    ms = jnp.mean(out * out, axis=-1, keepdims=True)
    o_ref[...] = (out * lax.rsqrt(ms + RMS_EPS) * g_ref[...] * (1.0 - lam_init)).astype(o_ref.dtype)


def diff_attention(proj, v_blk, lam_params, subln_g, lam_init, batch, seq, tq=512, tk=512):
    n = batch * seq
    tq = min(tq, seq)
    tk = min(tk, seq)
    nq = seq // tq
    h = ATT_HEADS
    return pl.pallas_call(
        functools.partial(_attn_body, tk=tk, lam_init=lam_init),
        grid=(batch, h, nq),
        in_specs=[pl.BlockSpec((4, ATT_HEAD_DIM), lambda b, hh, i: (0, 0)),
                  pl.BlockSpec((1, LANES), lambda b, hh, i: (0, 0)),
                  pl.BlockSpec((tq, LANES), lambda b, hh, i: (b * nq + i, hh)),
                  pl.BlockSpec((seq, LANES), lambda b, hh, i: (b, h + hh)),
                  pl.BlockSpec((seq, LANES), lambda b, hh, i: (b, v_blk + hh))],
        out_specs=pl.BlockSpec((tq, LANES), lambda b, hh, i: (b * nq + i, hh)),
        out_shape=jax.ShapeDtypeStruct((n, h * LANES), BF16),
        scratch_shapes=[pltpu.VMEM((2 * tq, LANES), BF16), pltpu.VMEM((2 * tq, LANES), F32),
                        pltpu.VMEM((2 * tq, 2 * LANES), F32),
                        pltpu.VMEM((2, 2 * tq, tk), F32)],
        compiler_params=_params("parallel", "parallel", "parallel"),
        name="diff_attention",
    )(lam_params.astype(F32), subln_g.reshape(1, LANES).astype(F32), proj, proj, proj)


HALO = 8


def _conv_body(prev_ref, cur_ref, next_ref, w_ref, b_ref, o_ref, *, n_seq_blocks):
    i = pl.program_id(1)
    ts = cur_ref.shape[0]
    prev = jnp.where(i > 0, prev_ref[...].astype(F32), 0.0)
    nxt = jnp.where(i < n_seq_blocks - 1, next_ref[...].astype(F32), 0.0)
    ext = jnp.concatenate([prev, cur_ref[...].astype(F32), nxt], axis=0)
    w = w_ref[...]
    acc = jnp.zeros(cur_ref.shape, F32) + b_ref[...]
    pad = SSM_CONV // 2
    for kk in range(SSM_CONV):
        start = HALO - pad + kk
        acc = acc + ext[start:start + ts, :] * w[kk:kk + 1, :]
    o_ref[...] = (acc * jax.nn.sigmoid(acc)).astype(o_ref.dtype)


def conv_silu(proj, col_off, conv_w, conv_b, batch, seq, ts=512, tc=512):
    n = batch * seq
    c = conv_w.shape[1]
    ts = min(ts, seq)
    nsb = seq // ts
    assert col_off % tc == 0 and c % tc == 0 and ts % HALO == 0
    cb = col_off // tc
    hb = ts // HALO
    last_halo = n // HALO - 1

    def prev_map(b, i, j):
        return (jnp.maximum((b * nsb + i) * hb - 1, 0), cb + j)

    def next_map(b, i, j):
        return (jnp.minimum((b * nsb + i + 1) * hb, last_halo), cb + j)

    return pl.pallas_call(
        functools.partial(_conv_body, n_seq_blocks=nsb),
        grid=(batch, nsb, c // tc),
        in_specs=[pl.BlockSpec((HALO, tc), prev_map),
                  pl.BlockSpec((ts, tc), lambda b, i, j: (b * nsb + i, cb + j)),
                  pl.BlockSpec((HALO, tc), next_map),
                  pl.BlockSpec((SSM_CONV, tc), lambda b, i, j: (0, j)),
                  pl.BlockSpec((1, tc), lambda b, i, j: (0, j))],
        out_specs=pl.BlockSpec((ts, tc), lambda b, i, j: (b * nsb + i, j)),
        out_shape=jax.ShapeDtypeStruct((n, c), BF16),
        compiler_params=_params("parallel", "parallel", "parallel"),
        name="conv_silu",
    )(proj, proj, proj, conv_w.astype(F32), conv_b.reshape(1, c).astype(F32))


def _softplus(x):
    return jnp.maximum(x, 0.0) + jnp.log1p(jnp.exp(-jnp.abs(x)))


def _ssd_body(x_ref, b_ref, c_ref, dt_ref, dtt_ref, bias_ref, biast_ref, alog_ref, alogt_ref,
              y_ref, state_ref, *, reverse):
    q = SSM_CHUNK
    hpg = SSM_HEADS // SSM_GROUPS
    gw = hpg * SSM_HEAD_DIM
    hi = lax.Precision.HIGHEST

    @pl.when(pl.program_id(1) == 0)
    def _():
        state_ref[...] = jnp.zeros(state_ref.shape, F32)

    row = lax.broadcasted_iota(jnp.int32, (q, q), 0)
    col = lax.broadcasted_iota(jnp.int32, (q, q), 1)
    keep = (col >= row) if reverse else (col <= row)
    incl = keep.astype(F32)
    incl_t = ((row >= col) if reverse else (row <= col)).astype(F32)

    dt = _softplus(dt_ref[...] + bias_ref[...])
    dtt = _softplus(dtt_ref[...] + biast_ref[...])
    a = dt * (-jnp.exp(alog_ref[...]))
    at = dtt * (-jnp.exp(alogt_ref[...]))
    cum = jnp.dot(incl, a, precision=hi, preferred_element_type=F32)
    cum_t = jnp.dot(at, incl_t, precision=hi, preferred_element_type=F32)
    total = jnp.sum(a, axis=0, keepdims=True)

    hid = lax.broadcasted_iota(jnp.int32, (SSM_HEADS, SSM_HEADS * SSM_HEAD_DIM), 0)
    lid = lax.broadcasted_iota(jnp.int32, (SSM_HEADS, SSM_HEADS * SSM_HEAD_DIM), 1)
    expand = (lid // SSM_HEAD_DIM == hid).astype(F32)

    def ex(v):
        return jnp.dot(v, expand, precision=hi, preferred_element_type=F32)

    xs = x_ref[...].astype(F32)
    xdt = xs * ex(dt)
    xdt_b = xdt.astype(BF16)
    xdec = (xdt * ex(jnp.exp(total - cum))).astype(BF16)
    dec_out = ex(jnp.exp(cum))
    dec_chunk = ex(jnp.exp(total))

    lane_g = lax.broadcasted_iota(jnp.int32, (1, gw), 1) // SSM_HEAD_DIM

    for g in range(SSM_GROUPS):
        bg = b_ref[:, g * SSM_STATE:(g + 1) * SSM_STATE]
        cg = c_ref[:, g * SSM_STATE:(g + 1) * SSM_STATE]
        cb = lax.dot_general(cg, bg, (((1,), (1,)), ((), ())), preferred_element_type=F32)
        xg = xdt_b[:, g * gw:(g + 1) * gw]
        ms, xbd = [], []
        for hh in range(hpg):
            head = g * hpg + hh
            diff = cum[:, head:head + 1] - cum_t[head:head + 1, :]
            decay = jnp.exp(jnp.where(keep, diff, -1e30))
            ms.append((cb * decay).astype(BF16))
            xbd.append(jnp.where(lane_g == hh, xg, jnp.zeros_like(xg)))
        m_cat = jnp.concatenate(ms, axis=1)
        x_bd = jnp.concatenate(xbd, axis=0)
        y_diag = jnp.dot(m_cat, x_bd, preferred_element_type=F32)

        st = state_ref[g]
        y_off = jnp.dot(cg, st.astype(BF16), preferred_element_type=F32) * dec_out[:, g * gw:(g + 1) * gw]
        y_ref[:, g * gw:(g + 1) * gw] = (y_diag + y_off).astype(y_ref.dtype)

        new = lax.dot_general(bg, xdec[:, g * gw:(g + 1) * gw], (((0,), (0,)), ((), ())),
                              preferred_element_type=F32)
        state_ref[g] = st * dec_chunk[:, g * gw:(g + 1) * gw] + new


def ssd_scan(xc, dt_raw, dt_bias, a_log, batch, seq, reverse):
    n = batch * seq
    q = SSM_CHUNK
    nc = seq // q
    hp = SSM_HEADS * SSM_HEAD_DIM
    gn = SSM_GROUPS * SSM_STATE
    assert xc.shape[1] == hp + 2 * gn and hp == 2 * gn

    def cidx(c):
        return nc - 1 - c if reverse else c

    return pl.pallas_call(
        functools.partial(_ssd_body, reverse=reverse),
        grid=(batch, nc),
        in_specs=[pl.BlockSpec((q, hp), lambda b, c: (b * nc + cidx(c), 0)),
                  pl.BlockSpec((q, gn), lambda b, c: (b * nc + cidx(c), 2)),
                  pl.BlockSpec((q, gn), lambda b, c: (b * nc + cidx(c), 3)),
                  pl.BlockSpec((q, SSM_HEADS), lambda b, c: (b * nc + cidx(c), 0)),
                  pl.BlockSpec((SSM_HEADS, q), lambda b, c: (0, b * nc + cidx(c))),
                  pl.BlockSpec((1, SSM_HEADS), lambda b, c: (0, 0)),
                  pl.BlockSpec((SSM_HEADS, 1), lambda b, c: (0, 0)),
                  pl.BlockSpec((1, SSM_HEADS), lambda b, c: (0, 0)),
                  pl.BlockSpec((SSM_HEADS, 1), lambda b, c: (0, 0))],
        out_specs=pl.BlockSpec((q, hp), lambda b, c: (b * nc + cidx(c), 0)),
        out_shape=jax.ShapeDtypeStruct((n, hp), BF16),
        scratch_shapes=[pltpu.VMEM((SSM_GROUPS, SSM_STATE, hp // SSM_GROUPS), F32)],
        compiler_params=_params("parallel", "arbitrary"),
        name="ssd_scan_bwd" if reverse else "ssd_scan_fwd",
    )(xc, xc, xc, dt_raw, dt_raw.T, dt_bias.reshape(1, -1).astype(F32),
      dt_bias.reshape(-1, 1).astype(F32), a_log.reshape(1, -1).astype(F32),
      a_log.reshape(-1, 1).astype(F32))


def _gated_norm_body(yf_ref, yb_ref, xs_ref, z_ref, d_ref, g_ref, o_ref):
    z = z_ref[...].astype(F32)
    y = (yf_ref[...].astype(F32) + yb_ref[...].astype(F32)
         + d_ref[...] * xs_ref[...].astype(F32)) * (z * jax.nn.sigmoid(z))
    ms = jnp.mean(y * y, axis=-1, keepdims=True)
    o_ref[...] = (y * lax.rsqrt(ms + RMS_EPS) * g_ref[...]).astype(o_ref.dtype)


def gated_norm(y_f, y_b, xc, proj, z_col_block, d_skip, norm_g, tm=256):
    n, c = y_f.shape
    tm = min(tm, n)
    row = lambda i: (i, 0)
    return pl.pallas_call(
        _gated_norm_body,
        grid=(n // tm,),
        in_specs=[pl.BlockSpec((tm, c), row), pl.BlockSpec((tm, c), row),
                  pl.BlockSpec((tm, c), row),
                  pl.BlockSpec((tm, c), lambda i: (i, z_col_block)),
                  pl.BlockSpec((1, c), lambda i: (0, 0)), pl.BlockSpec((1, c), lambda i: (0, 0))],
        out_specs=pl.BlockSpec((tm, c), row),
        out_shape=jax.ShapeDtypeStruct((n, c), BF16),
        compiler_params=_params("parallel"),
        name="gated_norm",
    )(y_f, y_b, xc, proj, jnp.repeat(d_skip.astype(F32), SSM_HEAD_DIM).reshape(1, c),
      norm_g.reshape(1, c).astype(F32))


def _merge_body(att_ref, m_ref, wa_ref, ws_ref, ga_ref, gs_ref, ba_ref, bs_ref, o_ref):
    a_out = jnp.dot(att_ref[...], wa_ref[...], preferred_element_type=F32)
    m_out = jnp.dot(m_ref[...], ws_ref[...], preferred_element_type=F32)
    g_att = jax.nn.sigmoid(ga_ref[...] + ba_ref[...])
    g_ssm = jax.nn.sigmoid(gs_ref[...] + bs_ref[...])
    o_ref[...] = (g_att * a_out + g_ssm * m_out).astype(o_ref.dtype)


def branch_merge(att, m, w_att, w_ssm, gate_logits, gate_b, tm=512, tn=512):
    n, d = att.shape[0], w_att.shape[1]
    tm = min(tm, n)
    nb = d // tn
    return pl.pallas_call(
        _merge_body,
        grid=(nb, n // tm),
        in_specs=[pl.BlockSpec((tm, att.shape[1]), lambda j, i: (i, 0)),
                  pl.BlockSpec((tm, m.shape[1]), lambda j, i: (i, 0)),
                  pl.BlockSpec((w_att.shape[0], tn), lambda j, i: (0, j)),
                  pl.BlockSpec((w_ssm.shape[0], tn), lambda j, i: (0, j)),
                  pl.BlockSpec((tm, tn), lambda j, i: (i, j)),
                  pl.BlockSpec((tm, tn), lambda j, i: (i, nb + j)),
                  pl.BlockSpec((1, tn), lambda j, i: (0, j)),
                  pl.BlockSpec((1, tn), lambda j, i: (0, nb + j))],
        out_specs=pl.BlockSpec((tm, tn), lambda j, i: (i, j)),
        out_shape=jax.ShapeDtypeStruct((n, d), BF16),
        compiler_params=_params("parallel", "parallel"),
        name="branch_merge",
    )(att, m, w_att, w_ssm, gate_logits, gate_logits, gate_b.reshape(1, -1).astype(F32),
      gate_b.reshape(1, -1).astype(F32))


def _peer_score_body(h_ref, wq_ref, keys_ref, o_ref):
    qry = jnp.dot(h_ref[...], wq_ref[...], preferred_element_type=F32).astype(BF16)
    for hc in range(2 * PEER_HEADS):
        qh = qry[:, hc * LANES:(hc + 1) * LANES]
        o_ref[hc] = lax.dot_general(keys_ref[hc], qh, (((1,), (1,)), ((), ())),
                                    preferred_element_type=F32)


def peer_scores(h, wq, keys, tm=512):
    n, d = h.shape
    tm = min(tm, n)
    nhc = 2 * PEER_HEADS
    return pl.pallas_call(
        _peer_score_body,
        grid=(n // tm,),
        in_specs=[pl.BlockSpec((tm, d), lambda i: (i, 0)),
                  pl.BlockSpec(wq.shape, lambda i: (0, 0)),
                  pl.BlockSpec(keys.shape, lambda i: (0, 0, 0))],
        out_specs=pl.BlockSpec((nhc, PEER_KEYS, tm), lambda i: (0, 0, i)),
        out_shape=jax.ShapeDtypeStruct((nhc, PEER_KEYS, n), F32),
        compiler_params=_params("parallel"),
        name="peer_scores",
    )(h, wq, keys)


def _top16_rows(x):
    r = x.shape[0]
    rid = lax.broadcasted_iota(jnp.int32, x.shape, 0)
    out = []
    for _ in range(PEER_TOPK):
        m = jnp.max(x, axis=0, keepdims=True)
        first = jnp.min(jnp.where(x == m, rid, r), axis=0, keepdims=True)
        x = jnp.where(rid == first, NEG_INF, x)
        out.append(m)
    return out


def _peer_route_body(sc_ref, st_ref):
    t = sc_ref.shape[2]
    sub = lax.broadcasted_iota(jnp.int32, (8, t), 0)
    for h in range(PEER_HEADS):
        va = _top16_rows(sc_ref[2 * h])
        vb = _top16_rows(sc_ref[2 * h + 1])

        def stack8(rows):
            acc = jnp.zeros((8, t), F32)
            for i, rw in enumerate(rows):
                acc = jnp.where(sub == i, rw, acc)
            return acc

        va_lo, va_hi = stack8(va[:8]), stack8(va[8:])
        vb_hi = stack8(vb[8:])
        groups = [va_lo + vb[0], va_hi + vb[0], va_lo + vb[1]]
        for qq, lim in ((2, 5), (3, 4), (4, 3), (5, 2), (6, 2), (7, 2)):
            groups.append(jnp.where(sub < lim, va_lo + vb[qq], NEG_INF))
        groups.append(va[0] + vb_hi)
        cand = jnp.concatenate(groups, axis=0)
        best = _top16_rows(cand)
        z = jnp.zeros((1, t), F32)
        for bv in best:
            z = z + jnp.exp(bv - best[0])
        st_ref[0, h:h + 1, :] = best[PEER_TOPK - 1]
        st_ref[1, h:h + 1, :] = va[0]
        st_ref[2, h:h + 1, :] = vb[0]
        st_ref[3, h:h + 1, :] = z


def peer_route(sc_t, tt=256):
    nhc, kk, n = sc_t.shape
    tt = min(tt, n)
    return pl.pallas_call(
        _peer_route_body,
        grid=(n // tt,),
        in_specs=[pl.BlockSpec((nhc, kk, tt), lambda i: (0, 0, i))],
        out_specs=pl.BlockSpec((4, PEER_HEADS, tt), lambda i: (0, 0, i)),
        out_shape=jax.ShapeDtypeStruct((4, PEER_HEADS, n), F32),
        compiler_params=_params("parallel"),
        name="peer_route",
    )(sc_t)


def _peer_mix_body(h_ref, u_ref, vt_ref, sc_ref, st_ref, x_ref, o_ref, acc_ref, ea_ref, eb_ref,
                   *, sb, tb):
    c = pl.program_id(1)
    ec = u_ref.shape[0]
    tm = h_ref.shape[0]
    ipc = ec // PEER_KEYS
    ips = sb // PEER_KEYS

    @pl.when(c == 0)
    def _():
        acc_ref[...] = jnp.zeros(acc_ref.shape, F32)
        for h in range(PEER_HEADS):
            half_inv_z = 0.5 / st_ref[3, h:h + 1, :]
            ea_ref[h] = jnp.exp(sc_ref[2 * h] - st_ref[1, h:h + 1, :]) * half_inv_z
            eb_ref[h] = jnp.exp(sc_ref[2 * h + 1] - st_ref[2, h:h + 1, :])

    for s_i in range(ec // sb):
        s_t = lax.dot_general(u_ref[s_i * sb:(s_i + 1) * sb, :], h_ref[...],
                              (((1,), (1,)), ((), ())), preferred_element_type=F32)
        w_rows = []
        for ii in range(ips):
            i = c * ipc + s_i * ips + ii
            w_cols = []
            a_rows = [sc_ref[2 * h, pl.ds(i, 1), :] for h in range(PEER_HEADS)]
            ea_rows = [ea_ref[h, pl.ds(i, 1), :] for h in range(PEER_HEADS)]
            for t_i in range(tm // tb):
                ln = slice(t_i * tb, (t_i + 1) * tb)
                gate = jnp.zeros((PEER_KEYS, tb), F32)
                for h in range(PEER_HEADS):
                    tau = st_ref[0, h:h + 1, ln]
                    ssum = a_rows[h][:, ln] + sc_ref[2 * h + 1, :, ln]
                    gate = gate + jnp.where(ssum >= tau, ea_rows[h][:, ln] * eb_ref[h, :, ln], 0.0)
                s_blk = s_t[ii * PEER_KEYS:(ii + 1) * PEER_KEYS, ln]
                act = s_blk * (1.0 + lax.erf(s_blk * (2.0 ** -0.5)))
                w_cols.append((act * gate).astype(BF16))
            w_rows.append(jnp.concatenate(w_cols, axis=1) if len(w_cols) > 1 else w_cols[0])
        w_t = jnp.concatenate(w_rows, axis=0) if ips > 1 else w_rows[0]
        acc_ref[...] += jnp.dot(vt_ref[:, s_i * sb:(s_i + 1) * sb], w_t,
                                preferred_element_type=F32)

    @pl.when(c == pl.num_programs(1) - 1)
    def _():
        o_ref[...] = x_ref[...] + acc_ref[...].T


def peer_mix(h, u_tab, vt_tab, sc_t, stats, x_res, tm=512, ec=1024, sb=512, tb=128):
    n, d = h.shape
    e = u_tab.shape[0]
    tm = min(tm, n)
    tb = min(tb, tm)
    nhc = 2 * PEER_HEADS
    assert e % ec == 0 and ec % sb == 0 and sb % PEER_KEYS == 0 and tm % tb == 0
    return pl.pallas_call(
        functools.partial(_peer_mix_body, sb=sb, tb=tb),
        grid=(n // tm, e // ec),
        in_specs=[pl.BlockSpec((tm, d), lambda t, c: (t, 0)),
                  pl.BlockSpec((ec, d), lambda t, c: (c, 0)),
                  pl.BlockSpec((d, ec), lambda t, c: (0, c)),
                  pl.BlockSpec((nhc, PEER_KEYS, tm), lambda t, c: (0, 0, t)),
                  pl.BlockSpec((4, PEER_HEADS, tm), lambda t, c: (0, 0, t)),
                  pl.BlockSpec((tm, d), lambda t, c: (t, 0))],
        out_specs=pl.BlockSpec((tm, d), lambda t, c: (t, 0)),
        out_shape=jax.ShapeDtypeStruct((n, d), F32),
        scratch_shapes=[pltpu.VMEM((d, tm), F32),
                        pltpu.VMEM((PEER_HEADS, PEER_KEYS, tm), F32),
                        pltpu.VMEM((PEER_HEADS, PEER_KEYS, tm), F32)],
        compiler_params=_params("parallel", "arbitrary"),
        name="peer_mix",
    )(h, u_tab, vt_tab, sc_t, stats, x_res)


def kernel(x, positions, norm1_g, w_in, gate_b, diff_lam, subln_g, w_att_br, conv_w, conv_b,
           dt_bias, a_log, d_skip, ssm_norm_g, w_ssm_br, w_out, norm2_g, peer_wq, peer_keys,
           peer_u, peer_v, final_g):
    batch, seq, d = x.shape
    depth = w_in.shape[0]
    n = batch * seq
    qk_cols = 2 * ATT_HEADS * 2 * ATT_HEAD_DIM
    qkv_cols = 3 * ATT_HEADS * 2 * ATT_HEAD_DIM
    z_cols = SSM_HEADS * SSM_HEAD_DIM
    xbc_cols = z_cols + 2 * SSM_GROUPS * SSM_STATE
    main_cols = qkv_cols + z_cols + xbc_cols
    dt_cols = 2 * SSM_HEADS

    cosf, sin_a, sin_b = rotary_tables(positions)
    q_scale = ATT_HEAD_DIM ** -0.5 * LOG2E
    xf = x.reshape(n, d)

    for l in range(depth):
        lam_init = 0.8 - 0.6 * math.exp(-0.3 * l)
        w_l = w_in[l]
        w_main = jnp.concatenate([w_l[:, :qk_cols], w_l[:, qkv_cols:main_cols],
                                  w_l[:, qk_cols:qkv_cols]], axis=1).astype(BF16)
        w_dt = w_l[:, main_cols:main_cols + dt_cols].astype(BF16)
        w_gate = w_l[:, main_cols + dt_cols:].astype(BF16)

        h1 = rmsnorm(xf, norm1_g[l])
        proj = inproj_rotary(h1, w_main, cosf, sin_a, sin_b, q_scale)
        dt_raw = matmul(h1, w_dt, F32)
        gate_logits = matmul(h1, w_gate, F32)

        att = diff_attention(proj, (qk_cols + z_cols + xbc_cols) // LANES, diff_lam[l],
                             subln_g[l], lam_init, batch, seq)

        xc = conv_silu(proj, qk_cols + z_cols, conv_w[l], conv_b[l], batch, seq)
        y_f = ssd_scan(xc, dt_raw[:, :SSM_HEADS], dt_bias[l, 0], a_log[l, 0], batch, seq, False)
        y_b = ssd_scan(xc, dt_raw[:, SSM_HEADS:], dt_bias[l, 1], a_log[l, 1], batch, seq, True)
        m = gated_norm(y_f, y_b, xc, proj, qk_cols // z_cols, d_skip[l], ssm_norm_g[l])

        merged = branch_merge(att, m, w_att_br[l].astype(BF16), w_ssm_br[l].astype(BF16),
                              gate_logits, gate_b[l])
        xf = matmul(merged, w_out[l].astype(BF16), F32, residual=xf)

        h2 = rmsnorm(xf, norm2_g[l])
        keys = peer_keys[l].reshape(2 * PEER_HEADS, PEER_KEYS, -1).astype(BF16)
        sc_t = peer_scores(h2, peer_wq[l].astype(BF16), keys)
        stats = peer_route(sc_t)
        xf = peer_mix(h2, peer_u[l].astype(BF16), peer_v[l].astype(BF16).T, sc_t, stats, xf)

    return rmsnorm(xf, final_g, out_dtype=x.dtype).reshape(batch, seq, d)
```

```python
import functools
import math

import jax
import jax.numpy as jnp
from jax import lax
from jax.experimental import pallas as pl
from jax.experimental.pallas import tpu as pltpu

F32 = jnp.float32
BF16 = jnp.bfloat16

LANES = 128
VMEM_LIMIT = 48 * 1024 * 1024

ATT_HEADS = 8
ATT_HEAD_DIM = 64
ROPE_DIM = 16
ROPE_THETA = 500000.0
SSM_HEADS = 32
SSM_HEAD_DIM = 64
SSM_GROUPS = 8
SSM_STATE = 128
SSM_CONV = 5
SSM_CHUNK = 128
PEER_HEADS = 8
PEER_KEYS = 128
PEER_TOPK = 16
RMS_EPS = 1e-6
NEG_INF = float("-inf")
LOG2E = 1.4426950408889634


def _params(*sem, flags=None):
    return pltpu.CompilerParams(dimension_semantics=sem, vmem_limit_bytes=VMEM_LIMIT, flags=flags)


def _rmsnorm_body(x_ref, g_ref, o_ref):
    x = x_ref[...].astype(F32)
    ms = jnp.mean(x * x, axis=-1, keepdims=True)
    o_ref[...] = (x * lax.rsqrt(ms + RMS_EPS) * g_ref[...]).astype(o_ref.dtype)


def rmsnorm(x, g, out_dtype=BF16, tm=512):
    n, d = x.shape
    tm = min(tm, n)
    return pl.pallas_call(
        _rmsnorm_body,
        grid=(n // tm,),
        in_specs=[pl.BlockSpec((tm, d), lambda i: (i, 0)),
                  pl.BlockSpec((1, d), lambda i: (0, 0))],
        out_specs=pl.BlockSpec((tm, d), lambda i: (i, 0)),
        out_shape=jax.ShapeDtypeStruct((n, d), out_dtype),
        compiler_params=_params("parallel"),
        name="rmsnorm",
    )(x, g.reshape(1, d).astype(F32))


def _mm_body(a_ref, w_ref, o_ref):
    o_ref[...] = jnp.dot(a_ref[...], w_ref[...], preferred_element_type=F32).astype(o_ref.dtype)


def _mm_res_body(a_ref, w_ref, r_ref, o_ref):
    acc = jnp.dot(a_ref[...], w_ref[...], preferred_element_type=F32)
    o_ref[...] = (r_ref[...].astype(F32) + acc).astype(o_ref.dtype)


def matmul(a, w, out_dtype, residual=None, tm=512, tn=1024):
    m, k = a.shape
    n = w.shape[1]
    tm = min(tm, m)
    tn = min(tn, n)
    assert m % tm == 0 and n % tn == 0
    in_specs = [pl.BlockSpec((tm, k), lambda j, i: (i, 0)),
                pl.BlockSpec((k, tn), lambda j, i: (0, j))]
    args = [a, w]
    body = _mm_body
    if residual is not None:
        in_specs.append(pl.BlockSpec((tm, tn), lambda j, i: (i, j)))
        args.append(residual)
        body = _mm_res_body
    return pl.pallas_call(
        body,
        grid=(n // tn, m // tm),
        in_specs=in_specs,
        out_specs=pl.BlockSpec((tm, tn), lambda j, i: (i, j)),
        out_shape=jax.ShapeDtypeStruct((m, n), out_dtype),
        compiler_params=_params("parallel", "parallel"),
        name="matmul",
    )(*args)


def _inproj_body(a_ref, w_ref, cos_ref, sa_ref, sb_ref, o_ref, *, q_scale):
    j = pl.program_id(0)
    acc = jnp.dot(a_ref[...], w_ref[...], preferred_element_type=F32)

    @pl.when(j >= 2)
    def _():
        o_ref[...] = acc.astype(o_ref.dtype)

    @pl.when(j < 2)
    def _():
        scale = jnp.where(j == 0, q_scale, 1.0).astype(F32)
        cosf = cos_ref[...] * scale
        sa = sa_ref[...] * scale
        sb = sb_ref[...] * scale
        tn = acc.shape[1]
        for g in range(tn // LANES):
            t = acc[:, g * LANES:(g + 1) * LANES]
            half = ROPE_DIM // 2
            r = (t * cosf + pltpu.roll(t, half, axis=1) * sa
                 + pltpu.roll(t, LANES - half, axis=1) * sb)
            o_ref[:, g * LANES:(g + 1) * LANES] = r.astype(o_ref.dtype)


def inproj_rotary(h, w, cosf, sin_a, sin_b, q_scale, tm=512, tn=1024):
    m, k = h.shape
    n = w.shape[1]
    tm = min(tm, m)
    assert m % tm == 0 and n % tn == 0
    tab = pl.BlockSpec((tm, LANES), lambda j, i: (i, 0))
    return pl.pallas_call(
        functools.partial(_inproj_body, q_scale=q_scale),
        grid=(n // tn, m // tm),
        in_specs=[pl.BlockSpec((tm, k), lambda j, i: (i, 0)),
                  pl.BlockSpec((k, tn), lambda j, i: (0, j)),
                  tab, tab, tab],
        out_specs=pl.BlockSpec((tm, tn), lambda j, i: (i, j)),
        out_shape=jax.ShapeDtypeStruct((m, n), BF16),
        compiler_params=_params("parallel", "parallel"),
        name="inproj_rotary",
    )(h, w, cosf, sin_a, sin_b)


def rotary_tables(positions):
    half = ROPE_DIM // 2
    inv_freq = ROPE_THETA ** (-jnp.arange(0, ROPE_DIM, 2, dtype=F32) / ROPE_DIM)
    ang = positions.reshape(-1).astype(F32)[:, None] * inv_freq
    cos, sin = jnp.cos(ang), jnp.sin(ang)
    n = ang.shape[0]
    one = jnp.ones((n, ATT_HEAD_DIM - ROPE_DIM), F32)
    zero8 = jnp.zeros((n, half), F32)
    zero = jnp.zeros((n, ATT_HEAD_DIM - ROPE_DIM), F32)
    cos64 = jnp.concatenate([cos, cos, one], axis=1)
    sa64 = jnp.concatenate([zero8, sin, zero], axis=1)
    sb64 = jnp.concatenate([-sin, zero8, zero], axis=1)
    dup = lambda t: jnp.concatenate([t, t], axis=1)
    return dup(cos64), dup(sa64), dup(sb64)


def _attn_body(lam_ref, g_ref, q_ref, k_ref, v_ref, o_ref, qs_ref, m_ref, acc_ref, s_ref,
               *, tk, lam_init):
    tq = q_ref.shape[0]
    s_len = k_ref.shape[0]
    lane = lax.broadcasted_iota(jnp.int32, (1, LANES), 1)
    first = lane < ATT_HEAD_DIM
    q = q_ref[...]
    zero = jnp.zeros_like(q)
    qs_ref[0:tq, :] = jnp.where(first, q, zero)
    qs_ref[tq:2 * tq, :] = jnp.where(first, zero, q)
    m_ref[...] = jnp.full(m_ref.shape, NEG_INF, F32)
    acc_ref[...] = jnp.zeros(acc_ref.shape, F32)
    ones = jnp.ones((tk, LANES), BF16)

    def scores(kc, slot):
        off = pl.multiple_of(kc * tk, tk)
        s_ref[slot] = lax.dot_general(qs_ref[...], k_ref[pl.ds(off, tk), :],
                                      (((1,), (1,)), ((), ())),
                                      preferred_element_type=F32)

    def update(kc, slot):
        off = pl.multiple_of(kc * tk, tk)
        v_ext = jnp.concatenate([v_ref[pl.ds(off, tk), :], ones], axis=1)
        s = s_ref[slot]
        m_old = m_ref[...]
        m_new = jnp.maximum(m_old, jnp.max(s, axis=-1, keepdims=True))
        alpha = jnp.exp2(m_old - m_new)
        p = jnp.concatenate(
            [jnp.exp2(s[:, j * LANES:(j + 1) * LANES] - m_new) for j in range(tk // LANES)], axis=1)
        pv = jnp.dot(p.astype(BF16), v_ext, preferred_element_type=F32)
        acc_ref[...] = jnp.concatenate([alpha, alpha], axis=1) * acc_ref[...] + pv
        m_ref[...] = m_new

    n_chunks = s_len // tk
    assert n_chunks % 2 == 0
    scores(0, 0)

    def step(i, carry):
        scores(2 * i + 1, 1)
        update(2 * i, 0)
        scores(jnp.minimum(2 * i + 2, n_chunks - 1), 0)
        update(2 * i + 1, 1)
        return carry

    lax.fori_loop(0, n_chunks // 2, step, 0)

    lp = lam_ref[...].astype(F32)
    lam = (jnp.exp(jnp.sum(lp[0:1] * lp[1:2], axis=-1, keepdims=True))
           - jnp.exp(jnp.sum(lp[2:3] * lp[3:4], axis=-1, keepdims=True)) + lam_init)
    a1 = acc_ref[0:tq, :]
    a2 = acc_ref[tq:2 * tq, :]
    out = a1[:, :LANES] / a1[:, LANES:] - lam * (a2[:, :LANES] / a2[:, LANES:])
    ms = jnp.mean(out * out, axis=-1, keepdims=True)
    o_ref[...] = (out * lax.rsqrt(ms + RMS_EPS) * g_ref[...] * (1.0 - lam_init)).astype(o_ref.dtype)


def diff_attention(proj, v_blk, lam_params, subln_g, lam_init, batch, seq, tq=512, tk=512):
    n = batch * seq
    tq = min(tq, seq)
    tk = min(tk, seq)
    nq = seq // tq
    h = ATT_HEADS
    return pl.pallas_call(
        functools.partial(_attn_body, tk=tk, lam_init=lam_init),
        grid=(batch, h, nq),
        in_specs=[pl.BlockSpec((4, ATT_HEAD_DIM), lambda b, hh, i: (0, 0)),
                  pl.BlockSpec((1, LANES), lambda b, hh, i: (0, 0)),
                  pl.BlockSpec((tq, LANES), lambda b, hh, i: (b * nq + i, hh)),
                  pl.BlockSpec((seq, LANES), lambda b, hh, i: (b, h + hh)),
                  pl.BlockSpec((seq, LANES), lambda b, hh, i: (b, v_blk + hh))],
        out_specs=pl.BlockSpec((tq, LANES), lambda b, hh, i: (b * nq + i, hh)),
        out_shape=jax.ShapeDtypeStruct((n, h * LANES), BF16),
        scratch_shapes=[pltpu.VMEM((2 * tq, LANES), BF16), pltpu.VMEM((2 * tq, LANES), F32),
                        pltpu.VMEM((2 * tq, 2 * LANES), F32),
                        pltpu.VMEM((2, 2 * tq, tk), F32)],
        compiler_params=_params("parallel", "parallel", "parallel"),
        name="diff_attention",
    )(lam_params.astype(F32), subln_g.reshape(1, LANES).astype(F32), proj, proj, proj)


HALO = 8


def _conv_body(prev_ref, cur_ref, next_ref, w_ref, b_ref, o_ref, *, n_seq_blocks):
    i = pl.program_id(1)
    ts = cur_ref.shape[0]
    prev = jnp.where(i > 0, prev_ref[...].astype(F32), 0.0)
    nxt = jnp.where(i < n_seq_blocks - 1, next_ref[...].astype(F32), 0.0)
    ext = jnp.concatenate([prev, cur_ref[...].astype(F32), nxt], axis=0)
    w = w_ref[...]
    acc = jnp.zeros(cur_ref.shape, F32) + b_ref[...]
    pad = SSM_CONV // 2
    for kk in range(SSM_CONV):
        start = HALO - pad + kk
        acc = acc + ext[start:start + ts, :] * w[kk:kk + 1, :]
    o_ref[...] = (acc * jax.nn.sigmoid(acc)).astype(o_ref.dtype)


def conv_silu(proj, col_off, conv_w, conv_b, batch, seq, ts=512, tc=512):
    n = batch * seq
    c = conv_w.shape[1]
    ts = min(ts, seq)
    nsb = seq // ts
    assert col_off % tc == 0 and c % tc == 0 and ts % HALO == 0
    cb = col_off // tc
    hb = ts // HALO
    last_halo = n // HALO - 1

    def prev_map(b, i, j):
        return (jnp.maximum((b * nsb + i) * hb - 1, 0), cb + j)

    def next_map(b, i, j):
        return (jnp.minimum((b * nsb + i + 1) * hb, last_halo), cb + j)

    return pl.pallas_call(
        functools.partial(_conv_body, n_seq_blocks=nsb),
        grid=(batch, nsb, c // tc),
        in_specs=[pl.BlockSpec((HALO, tc), prev_map),
                  pl.BlockSpec((ts, tc), lambda b, i, j: (b * nsb + i, cb + j)),
                  pl.BlockSpec((HALO, tc), next_map),
                  pl.BlockSpec((SSM_CONV, tc), lambda b, i, j: (0, j)),
                  pl.BlockSpec((1, tc), lambda b, i, j: (0, j))],
        out_specs=pl.BlockSpec((ts, tc), lambda b, i, j: (b * nsb + i, j)),
        out_shape=jax.ShapeDtypeStruct((n, c), BF16),
        compiler_params=_params("parallel", "parallel", "parallel"),
        name="conv_silu",
    )(proj, proj, proj, conv_w.astype(F32), conv_b.reshape(1, c).astype(F32))


def _softplus(x):
    return jnp.maximum(x, 0.0) + jnp.log1p(jnp.exp(-jnp.abs(x)))


def _ssd_body(x_ref, b_ref, c_ref, dt_ref, dtt_ref, bias_ref, biast_ref, alog_ref, alogt_ref,
              y_ref, state_ref, *, reverse):
    q = SSM_CHUNK
    hpg = SSM_HEADS // SSM_GROUPS
    gw = hpg * SSM_HEAD_DIM
    hi = lax.Precision.HIGHEST

    @pl.when(pl.program_id(1) == 0)
    def _():
        state_ref[...] = jnp.zeros(state_ref.shape, F32)

    row = lax.broadcasted_iota(jnp.int32, (q, q), 0)
    col = lax.broadcasted_iota(jnp.int32, (q, q), 1)
    keep = (col >= row) if reverse else (col <= row)
    incl = keep.astype(BF16)
    incl_t = ((row >= col) if reverse else (row <= col)).astype(BF16)

    def split(v):
        v_hi = v.astype(BF16)
        return v_hi, (v - v_hi.astype(F32)).astype(BF16)

    dt = _softplus(dt_ref[...] + bias_ref[...])
    dtt = _softplus(dtt_ref[...] + biast_ref[...])
    a = dt * (-jnp.exp(alog_ref[...]))
    at = dtt * (-jnp.exp(alogt_ref[...]))
    a_hi, a_lo = split(a)
    at_hi, at_lo = split(at)
    cum = (jnp.dot(incl, a_hi, preferred_element_type=F32)
           + jnp.dot(incl, a_lo, preferred_element_type=F32))
    cum_t = (jnp.dot(at_hi, incl_t, preferred_element_type=F32)
             + jnp.dot(at_lo, incl_t, preferred_element_type=F32))
    total = jnp.sum(a, axis=0, keepdims=True)

    hid = lax.broadcasted_iota(jnp.int32, (SSM_HEADS, SSM_HEADS * SSM_HEAD_DIM), 0)
    lid = lax.broadcasted_iota(jnp.int32, (SSM_HEADS, SSM_HEADS * SSM_HEAD_DIM), 1)
    expand = (lid // SSM_HEAD_DIM == hid).astype(BF16)

    def ex(v):
        v_hi, v_lo = split(v)
        return (jnp.dot(v_hi, expand, preferred_element_type=F32)
                + jnp.dot(v_lo, expand, preferred_element_type=F32))

    xdec = (x_ref[...].astype(F32) * ex(dt * jnp.exp(total - cum))).astype(BF16)
    dec_out = ex(jnp.exp(cum))
    dec_chunk = ex(jnp.exp(total))

    lane_g = lax.broadcasted_iota(jnp.int32, (1, gw), 1) // SSM_HEAD_DIM

    for g in range(SSM_GROUPS):
        bg = b_ref[:, g * SSM_STATE:(g + 1) * SSM_STATE]
        cg = c_ref[:, g * SSM_STATE:(g + 1) * SSM_STATE]
        cb = lax.dot_general(cg, bg, (((1,), (1,)), ((), ())), preferred_element_type=F32)
        xg = x_ref[:, g * gw:(g + 1) * gw]
        ms, xbd = [], []
        for hh in range(hpg):
            head = g * hpg + hh
            diff = cum[:, head:head + 1] - cum_t[head:head + 1, :]
            decay = jnp.exp(jnp.where(keep, diff, -1e30)) * dtt[head:head + 1, :]
            ms.append((cb * decay).astype(BF16))
            xbd.append(jnp.where(lane_g == hh, xg, jnp.zeros_like(xg)))
        m_cat = jnp.concatenate(ms, axis=1)
        x_bd = jnp.concatenate(xbd, axis=0)
        y_diag = jnp.dot(m_cat, x_bd, preferred_element_type=F32)

        st = state_ref[g]
        y_off = jnp.dot(cg, st.astype(BF16), preferred_element_type=F32) * dec_out[:, g * gw:(g + 1) * gw]
        y_ref[:, g * gw:(g + 1) * gw] = (y_diag + y_off).astype(y_ref.dtype)

        new = lax.dot_general(bg, xdec[:, g * gw:(g + 1) * gw], (((0,), (0,)), ((), ())),
                              preferred_element_type=F32)
        state_ref[g] = st * dec_chunk[:, g * gw:(g + 1) * gw] + new


def ssd_scan(xc, dt_raw, dt_bias, a_log, batch, seq, reverse):
    n = batch * seq
    q = SSM_CHUNK
    nc = seq // q
    hp = SSM_HEADS * SSM_HEAD_DIM
    gn = SSM_GROUPS * SSM_STATE
    assert xc.shape[1] == hp + 2 * gn and hp == 2 * gn

    def cidx(c):
        return nc - 1 - c if reverse else c

    return pl.pallas_call(
        functools.partial(_ssd_body, reverse=reverse),
        grid=(batch, nc),
        in_specs=[pl.BlockSpec((q, hp), lambda b, c: (b * nc + cidx(c), 0)),
                  pl.BlockSpec((q, gn), lambda b, c: (b * nc + cidx(c), 2)),
                  pl.BlockSpec((q, gn), lambda b, c: (b * nc + cidx(c), 3)),
                  pl.BlockSpec((q, SSM_HEADS), lambda b, c: (b * nc + cidx(c), 0)),
                  pl.BlockSpec((SSM_HEADS, q), lambda b, c: (0, b * nc + cidx(c))),
                  pl.BlockSpec((1, SSM_HEADS), lambda b, c: (0, 0)),
                  pl.BlockSpec((SSM_HEADS, 1), lambda b, c: (0, 0)),
                  pl.BlockSpec((1, SSM_HEADS), lambda b, c: (0, 0)),
                  pl.BlockSpec((SSM_HEADS, 1), lambda b, c: (0, 0))],
        out_specs=pl.BlockSpec((q, hp), lambda b, c: (b * nc + cidx(c), 0)),
        out_shape=jax.ShapeDtypeStruct((n, hp), BF16),
        scratch_shapes=[pltpu.VMEM((SSM_GROUPS, SSM_STATE, hp // SSM_GROUPS), F32)],
        compiler_params=_params("parallel", "arbitrary"),
        name="ssd_scan_bwd" if reverse else "ssd_scan_fwd",
    )(xc, xc, xc, dt_raw, dt_raw.T, dt_bias.reshape(1, -1).astype(F32),
      dt_bias.reshape(-1, 1).astype(F32), a_log.reshape(1, -1).astype(F32),
      a_log.reshape(-1, 1).astype(F32))


def _gated_norm_body(yf_ref, yb_ref, xs_ref, z_ref, d_ref, g_ref, o_ref):
    z = z_ref[...].astype(F32)
    y = (yf_ref[...].astype(F32) + yb_ref[...].astype(F32)
         + d_ref[...] * xs_ref[...].astype(F32)) * (z * jax.nn.sigmoid(z))
    ms = jnp.mean(y * y, axis=-1, keepdims=True)
    o_ref[...] = (y * lax.rsqrt(ms + RMS_EPS) * g_ref[...]).astype(o_ref.dtype)


def gated_norm(y_f, y_b, xc, proj, z_col_block, d_skip, norm_g, tm=256):
    n, c = y_f.shape
    tm = min(tm, n)
    row = lambda i: (i, 0)
    return pl.pallas_call(
        _gated_norm_body,
        grid=(n // tm,),
        in_specs=[pl.BlockSpec((tm, c), row), pl.BlockSpec((tm, c), row),
                  pl.BlockSpec((tm, c), row),
                  pl.BlockSpec((tm, c), lambda i: (i, z_col_block)),
                  pl.BlockSpec((1, c), lambda i: (0, 0)), pl.BlockSpec((1, c), lambda i: (0, 0))],
        out_specs=pl.BlockSpec((tm, c), row),
        out_shape=jax.ShapeDtypeStruct((n, c), BF16),
        compiler_params=_params("parallel"),
        name="gated_norm",
    )(y_f, y_b, xc, proj, jnp.repeat(d_skip.astype(F32), SSM_HEAD_DIM).reshape(1, c),
      norm_g.reshape(1, c).astype(F32))


def _merge_body(att_ref, m_ref, wa_ref, ws_ref, ga_ref, gs_ref, ba_ref, bs_ref, o_ref):
    a_out = jnp.dot(att_ref[...], wa_ref[...], preferred_element_type=F32)
    m_out = jnp.dot(m_ref[...], ws_ref[...], preferred_element_type=F32)
    g_att = jax.nn.sigmoid(ga_ref[...] + ba_ref[...])
    g_ssm = jax.nn.sigmoid(gs_ref[...] + bs_ref[...])
    o_ref[...] = (g_att * a_out + g_ssm * m_out).astype(o_ref.dtype)


def branch_merge(att, m, w_att, w_ssm, gate_logits, gate_b, tm=512, tn=512):
    n, d = att.shape[0], w_att.shape[1]
    tm = min(tm, n)
    nb = d // tn
    return pl.pallas_call(
        _merge_body,
        grid=(nb, n // tm),
        in_specs=[pl.BlockSpec((tm, att.shape[1]), lambda j, i: (i, 0)),
                  pl.BlockSpec((tm, m.shape[1]), lambda j, i: (i, 0)),
                  pl.BlockSpec((w_att.shape[0], tn), lambda j, i: (0, j)),
                  pl.BlockSpec((w_ssm.shape[0], tn), lambda j, i: (0, j)),
                  pl.BlockSpec((tm, tn), lambda j, i: (i, j)),
                  pl.BlockSpec((tm, tn), lambda j, i: (i, nb + j)),
                  pl.BlockSpec((1, tn), lambda j, i: (0, j)),
                  pl.BlockSpec((1, tn), lambda j, i: (0, nb + j))],
        out_specs=pl.BlockSpec((tm, tn), lambda j, i: (i, j)),
        out_shape=jax.ShapeDtypeStruct((n, d), BF16),
        compiler_params=_params("parallel", "parallel"),
        name="branch_merge",
    )(att, m, w_att, w_ssm, gate_logits, gate_logits, gate_b.reshape(1, -1).astype(F32),
      gate_b.reshape(1, -1).astype(F32))


def _peer_score_body(h_ref, wq_ref, keys_ref, o_ref):
    qry = jnp.dot(h_ref[...], wq_ref[...], preferred_element_type=F32).astype(BF16)
    for hc in range(2 * PEER_HEADS):
        qh = qry[:, hc * LANES:(hc + 1) * LANES]
        o_ref[hc] = lax.dot_general(keys_ref[hc], qh, (((1,), (1,)), ((), ())),
                                    preferred_element_type=F32)


def peer_scores(h, wq, keys, tm=512):
    n, d = h.shape
    tm = min(tm, n)
    nhc = 2 * PEER_HEADS
    return pl.pallas_call(
        _peer_score_body,
        grid=(n // tm,),
        in_specs=[pl.BlockSpec((tm, d), lambda i: (i, 0)),
                  pl.BlockSpec(wq.shape, lambda i: (0, 0)),
                  pl.BlockSpec(keys.shape, lambda i: (0, 0, 0))],
        out_specs=pl.BlockSpec((nhc, PEER_KEYS, tm), lambda i: (0, 0, i)),
        out_shape=jax.ShapeDtypeStruct((nhc, PEER_KEYS, n), F32),
        compiler_params=_params("parallel"),
        name="peer_scores",
    )(h, wq, keys)


def _top16_rows(x):
    r = x.shape[0]
    rid = lax.broadcasted_iota(jnp.int32, x.shape, 0)
    out = []
    for _ in range(PEER_TOPK):
        m = jnp.max(x, axis=0, keepdims=True)
        first = jnp.min(jnp.where(x == m, rid, r), axis=0, keepdims=True)
        x = jnp.where(rid == first, NEG_INF, x)
        out.append(m)
    return out


def _batcher_pairs(n):
    pairs, p = [], 1
    while p < n:
        k = p
        while k >= 1:
            for j in range(k % p, n - k, 2 * k):
                for i in range(min(k, n - j - k)):
                    if (i + j) // (2 * p) == (i + j + k) // (2 * p):
                        pairs.append((i + j, i + j + k))
            k //= 2
        p *= 2
    return pairs


def _top16_sorted(x_ref, idx):
    n = PEER_TOPK
    v = [x_ref[idx, 8 * k:8 * k + 8, :] for k in range(n)]

    def cmpx(i, j):
        v[i], v[j] = jnp.maximum(v[i], v[j]), jnp.minimum(v[i], v[j])

    for i, j in _batcher_pairs(n):
        cmpx(i, j)
    for shift in (4, 2, 1):
        other = [pltpu.roll(vk, shift, axis=0) for vk in v]
        v = [jnp.maximum(v[k], other[n - 1 - k]) for k in range(n)]
        d = n // 2
        while d >= 1:
            for k in range(n):
                if k & d == 0:
                    cmpx(k, k + d)
            d //= 2
    return v


def _peer_route_body(sc_ref, st_ref):
    t = sc_ref.shape[2]
    sub = lax.broadcasted_iota(jnp.int32, (8, t), 0)
    for h in range(PEER_HEADS):
        va = _top16_sorted(sc_ref, 2 * h)
        vb = _top16_sorted(sc_ref, 2 * h + 1)

        def stack8(rows):
            acc = jnp.zeros((8, t), F32)
            for i, rw in enumerate(rows):
                acc = jnp.where(sub == i, rw, acc)
            return acc

        va_lo, va_hi = stack8(va[:8]), stack8(va[8:])
        vb_hi = stack8(vb[8:])
        groups = [va_lo + vb[0], va_hi + vb[0], va_lo + vb[1]]
        for qq, lim in ((2, 5), (3, 4), (4, 3), (5, 2), (6, 2), (7, 2)):
            groups.append(jnp.where(sub < lim, va_lo + vb[qq], NEG_INF))
        groups.append(va[0] + vb_hi)
        cand = jnp.concatenate(groups, axis=0)
        best = _top16_rows(cand)
        z = jnp.zeros((1, t), F32)
        for bv in best:
            z = z + jnp.exp(bv - best[0])
        st_ref[0, h:h + 1, :] = best[PEER_TOPK - 1]
        st_ref[1, h:h + 1, :] = va[0][0:1, :]
        st_ref[2, h:h + 1, :] = vb[0][0:1, :]
        st_ref[3, h:h + 1, :] = z


def peer_route(sc_t, tt=256):
    nhc, kk, n = sc_t.shape
    tt = min(tt, n)
    return pl.pallas_call(
        _peer_route_body,
        grid=(n // tt,),
        in_specs=[pl.BlockSpec((nhc, kk, tt), lambda i: (0, 0, i))],
        out_specs=pl.BlockSpec((4, PEER_HEADS, tt), lambda i: (0, 0, i)),
        out_shape=jax.ShapeDtypeStruct((4, PEER_HEADS, n), F32),
        compiler_params=_params("parallel"),
        name="peer_route",
    )(sc_t)


PEER_PAIR = 2 * PEER_KEYS


def _peer_mix_body(h_ref, u_ref, vt_ref, sc_ref, st_ref, x_ref, o_ref,
                   acc_ref, ea_ref, eb_ref, s00_ref, s01_ref, s10_ref, s11_ref, w0_ref, w1_ref, *, tb):
    c = pl.program_id(1)
    npair = u_ref.shape[0]
    tm = h_ref.shape[0]
    ipp = PEER_PAIR // PEER_KEYS
    s_slots = ((s00_ref, s01_ref), (s10_ref, s11_ref))
    w_slots = (w0_ref, w1_ref)

    @pl.when(c == 0)
    def _():
        acc_ref[...] = jnp.zeros(acc_ref.shape, F32)
        for h in range(PEER_HEADS):
            half_inv_z = 0.5 / st_ref[3, h:h + 1, :]
            ea_ref[h] = jnp.exp(sc_ref[2 * h] - st_ref[1, h:h + 1, :]) * half_inv_z
            eb_ref[h] = jnp.exp(sc_ref[2 * h + 1] - st_ref[2, h:h + 1, :])

    def score(pair, slot):
        for ii in range(ipp):
            s_slots[slot][ii][...] = lax.dot_general(
                u_ref[pair, ii * PEER_KEYS:(ii + 1) * PEER_KEYS, :], h_ref[...],
                (((1,), (1,)), ((), ())), preferred_element_type=F32)

    def gate_act(pair, slot):
        for ii in range(ipp):
            i = (c * npair + pair) * ipp + ii
            a_rows = [sc_ref[2 * h, pl.ds(i, 1), :] for h in range(PEER_HEADS)]
            ea_rows = [ea_ref[h, pl.ds(i, 1), :] for h in range(PEER_HEADS)]
            rows = slice(ii * PEER_KEYS, (ii + 1) * PEER_KEYS)
            for t_i in range(tm // tb):
                ln = slice(t_i * tb, (t_i + 1) * tb)
                gate = jnp.zeros((PEER_KEYS, tb), F32)
                for h in range(PEER_HEADS):
                    tau = st_ref[0, h:h + 1, ln]
                    ssum = a_rows[h][:, ln] + sc_ref[2 * h + 1, :, ln]
                    gate = gate + jnp.where(ssum >= tau, ea_rows[h][:, ln] * eb_ref[h, :, ln], 0.0)
                s_blk = s_slots[slot][ii][:, ln]
                act = s_blk * (1.0 + lax.erf(s_blk * (2.0 ** -0.5)))
                w_slots[slot][rows, ln] = (act * gate).astype(BF16)

    def mix(pair, slot):
        acc_ref[...] += jnp.dot(vt_ref[pair], w_slots[slot][...],
                                preferred_element_type=F32)

    score(0, 0)

    def trip(q, carry):
        score(2 * q + 1, 1)
        gate_act(2 * q, 0)
        mix(2 * q, 0)
        score(jnp.minimum(2 * q + 2, npair - 1), 0)
        gate_act(2 * q + 1, 1)
        mix(2 * q + 1, 1)
        return carry

    lax.fori_loop(0, npair // 2, trip, 0)

    @pl.when(c == pl.num_programs(1) - 1)
    def _():
        o_ref[...] = x_ref[...] + acc_ref[...].T


def peer_mix(h, u3, vt3, sc_t, stats, x_res, tm=256, npair=8, tb=128):
    n, d = h.shape
    tm = min(tm, n)
    tb = min(tb, tm)
    nhc = 2 * PEER_HEADS
    assert u3.shape[0] % npair == 0 and npair % 2 == 0 and tm % tb == 0
    return pl.pallas_call(
        functools.partial(_peer_mix_body, tb=tb),
        grid=(n // tm, u3.shape[0] // npair),
        in_specs=[pl.BlockSpec((tm, d), lambda t, c: (t, 0)),
                  pl.BlockSpec((npair, PEER_PAIR, d), lambda t, c: (c, 0, 0)),
                  pl.BlockSpec((npair, d, PEER_PAIR), lambda t, c: (c, 0, 0)),
                  pl.BlockSpec((nhc, PEER_KEYS, tm), lambda t, c: (0, 0, t)),
                  pl.BlockSpec((4, PEER_HEADS, tm), lambda t, c: (0, 0, t)),
                  pl.BlockSpec((tm, d), lambda t, c: (t, 0))],
        out_specs=pl.BlockSpec((tm, d), lambda t, c: (t, 0)),
        out_shape=jax.ShapeDtypeStruct((n, d), F32),
        scratch_shapes=[pltpu.VMEM((d, tm), F32),
                        pltpu.VMEM((PEER_HEADS, PEER_KEYS, tm), F32),
                        pltpu.VMEM((PEER_HEADS, PEER_KEYS, tm), F32),
                        pltpu.VMEM((PEER_KEYS, tm), F32), pltpu.VMEM((PEER_KEYS, tm), F32),
                        pltpu.VMEM((PEER_KEYS, tm), F32), pltpu.VMEM((PEER_KEYS, tm), F32),
                        pltpu.VMEM((PEER_PAIR, tm), BF16), pltpu.VMEM((PEER_PAIR, tm), BF16)],
        compiler_params=_params("parallel", "arbitrary"),
        name="peer_mix",
    )(h, u3, vt3, sc_t, stats, x_res)


def kernel(x, positions, norm1_g, w_in, gate_b, diff_lam, subln_g, w_att_br, conv_w, conv_b,
           dt_bias, a_log, d_skip, ssm_norm_g, w_ssm_br, w_out, norm2_g, peer_wq, peer_keys,
           peer_u, peer_v, final_g):
    batch, seq, d = x.shape
    depth = w_in.shape[0]
    n = batch * seq
    qk_cols = 2 * ATT_HEADS * 2 * ATT_HEAD_DIM
    qkv_cols = 3 * ATT_HEADS * 2 * ATT_HEAD_DIM
    z_cols = SSM_HEADS * SSM_HEAD_DIM
    xbc_cols = z_cols + 2 * SSM_GROUPS * SSM_STATE
    main_cols = qkv_cols + z_cols + xbc_cols
    dt_cols = 2 * SSM_HEADS

    cosf, sin_a, sin_b = rotary_tables(positions)
    q_scale = ATT_HEAD_DIM ** -0.5 * LOG2E
    xf = x.reshape(n, d)

    for l in range(depth):
        lam_init = 0.8 - 0.6 * math.exp(-0.3 * l)
        w_l = w_in[l]
        w_main = jnp.concatenate([w_l[:, :qk_cols], w_l[:, qkv_cols:main_cols],
                                  w_l[:, qk_cols:qkv_cols]], axis=1).astype(BF16)
        w_dt = w_l[:, main_cols:main_cols + dt_cols].astype(BF16)
        w_gate = w_l[:, main_cols + dt_cols:].astype(BF16)

        h1 = rmsnorm(xf, norm1_g[l])
        proj = inproj_rotary(h1, w_main, cosf, sin_a, sin_b, q_scale)
        dt_raw = matmul(h1, w_dt, F32)
        gate_logits = matmul(h1, w_gate, F32)

        att = diff_attention(proj, (qk_cols + z_cols + xbc_cols) // LANES, diff_lam[l],
                             subln_g[l], lam_init, batch, seq)

        xc = conv_silu(proj, qk_cols + z_cols, conv_w[l], conv_b[l], batch, seq)
        y_f = ssd_scan(xc, dt_raw[:, :SSM_HEADS], dt_bias[l, 0], a_log[l, 0], batch, seq, False)
        y_b = ssd_scan(xc, dt_raw[:, SSM_HEADS:], dt_bias[l, 1], a_log[l, 1], batch, seq, True)
        m = gated_norm(y_f, y_b, xc, proj, qk_cols // z_cols, d_skip[l], ssm_norm_g[l])

        merged = branch_merge(att, m, w_att_br[l].astype(BF16), w_ssm_br[l].astype(BF16),
                              gate_logits, gate_b[l])
        xf = matmul(merged, w_out[l].astype(BF16), F32, residual=xf)

        h2 = rmsnorm(xf, norm2_g[l])
        keys = peer_keys[l].reshape(2 * PEER_HEADS, PEER_KEYS, -1).astype(BF16)
        sc_t = peer_scores(h2, peer_wq[l].astype(BF16), keys)
        stats = peer_route(sc_t)
        u3 = peer_u[l].astype(BF16).reshape(-1, PEER_PAIR, d)
        vt3 = peer_v[l].astype(BF16).reshape(-1, PEER_PAIR, d).transpose(0, 2, 1)
        xf = peer_mix(h2, u3, vt3, sc_t, stats, xf)

    return rmsnorm(xf, final_g, out_dtype=x.dtype).reshape(batch, seq, d)
```

```python
import functools
import math

import jax
import jax.numpy as jnp
from jax import lax
from jax.experimental import pallas as pl
from jax.experimental.pallas import tpu as pltpu

F32 = jnp.float32
BF16 = jnp.bfloat16

LANES = 128
VMEM_LIMIT = 48 * 1024 * 1024

ATT_HEADS = 8
ATT_HEAD_DIM = 64
ROPE_DIM = 16
ROPE_THETA = 500000.0
SSM_HEADS = 32
SSM_HEAD_DIM = 64
SSM_GROUPS = 8
SSM_STATE = 128
SSM_CONV = 5
SSM_CHUNK = 128
PEER_HEADS = 8
PEER_KEYS = 128
PEER_TOPK = 16
RMS_EPS = 1e-6
NEG_INF = float("-inf")
LOG2E = 1.4426950408889634


def _params(*sem, flags=None):
    return pltpu.CompilerParams(dimension_semantics=sem, vmem_limit_bytes=VMEM_LIMIT, flags=flags)


def _rmsnorm_body(x_ref, g_ref, o_ref):
    x = x_ref[...].astype(F32)
    ms = jnp.mean(x * x, axis=-1, keepdims=True)
    o_ref[...] = (x * lax.rsqrt(ms + RMS_EPS) * g_ref[...]).astype(o_ref.dtype)


def rmsnorm(x, g, out_dtype=BF16, tm=512):
    n, d = x.shape
    tm = min(tm, n)
    return pl.pallas_call(
        _rmsnorm_body,
        grid=(n // tm,),
        in_specs=[pl.BlockSpec((tm, d), lambda i: (i, 0)),
                  pl.BlockSpec((1, d), lambda i: (0, 0))],
        out_specs=pl.BlockSpec((tm, d), lambda i: (i, 0)),
        out_shape=jax.ShapeDtypeStruct((n, d), out_dtype),
        compiler_params=_params("parallel"),
        name="rmsnorm",
    )(x, g.reshape(1, d).astype(F32))


def _mm_body(a_ref, w_ref, o_ref):
    o_ref[...] = jnp.dot(a_ref[...], w_ref[...], preferred_element_type=F32).astype(o_ref.dtype)


def _mm_res_body(a_ref, w_ref, r_ref, o_ref):
    acc = jnp.dot(a_ref[...], w_ref[...], preferred_element_type=F32)
    o_ref[...] = (r_ref[...].astype(F32) + acc).astype(o_ref.dtype)


def matmul(a, w, out_dtype, residual=None, tm=512, tn=1024):
    m, k = a.shape
    n = w.shape[1]
    tm = min(tm, m)
    tn = min(tn, n)
    assert m % tm == 0 and n % tn == 0
    in_specs = [pl.BlockSpec((tm, k), lambda j, i: (i, 0)),
                pl.BlockSpec((k, tn), lambda j, i: (0, j))]
    args = [a, w]
    body = _mm_body
    if residual is not None:
        in_specs.append(pl.BlockSpec((tm, tn), lambda j, i: (i, j)))
        args.append(residual)
        body = _mm_res_body
    return pl.pallas_call(
        body,
        grid=(n // tn, m // tm),
        in_specs=in_specs,
        out_specs=pl.BlockSpec((tm, tn), lambda j, i: (i, j)),
        out_shape=jax.ShapeDtypeStruct((m, n), out_dtype),
        compiler_params=_params("parallel", "parallel"),
        name="matmul",
    )(*args)


def _inproj_body(a_ref, w_ref, cos_ref, sa_ref, sb_ref, o_ref, *, q_scale):
    j = pl.program_id(0)
    acc = jnp.dot(a_ref[...], w_ref[...], preferred_element_type=F32)

    @pl.when(j >= 2)
    def _():
        o_ref[...] = acc.astype(o_ref.dtype)

    @pl.when(j < 2)
    def _():
        scale = jnp.where(j == 0, q_scale, 1.0).astype(F32)
        cosf = cos_ref[...] * scale
        sa = sa_ref[...] * scale
        sb = sb_ref[...] * scale
        tn = acc.shape[1]
        for g in range(tn // LANES):
            t = acc[:, g * LANES:(g + 1) * LANES]
            half = ROPE_DIM // 2
            r = (t * cosf + pltpu.roll(t, half, axis=1) * sa
                 + pltpu.roll(t, LANES - half, axis=1) * sb)
            o_ref[:, g * LANES:(g + 1) * LANES] = r.astype(o_ref.dtype)


def inproj_rotary(h, w, cosf, sin_a, sin_b, q_scale, tm=512, tn=1024):
    m, k = h.shape
    n = w.shape[1]
    tm = min(tm, m)
    assert m % tm == 0 and n % tn == 0
    tab = pl.BlockSpec((tm, LANES), lambda j, i: (i, 0))
    return pl.pallas_call(
        functools.partial(_inproj_body, q_scale=q_scale),
        grid=(n // tn, m // tm),
        in_specs=[pl.BlockSpec((tm, k), lambda j, i: (i, 0)),
                  pl.BlockSpec((k, tn), lambda j, i: (0, j)),
                  tab, tab, tab],
        out_specs=pl.BlockSpec((tm, tn), lambda j, i: (i, j)),
        out_shape=jax.ShapeDtypeStruct((m, n), BF16),
        compiler_params=_params("parallel", "parallel"),
        name="inproj_rotary",
    )(h, w, cosf, sin_a, sin_b)


def rotary_tables(positions):
    half = ROPE_DIM // 2
    inv_freq = ROPE_THETA ** (-jnp.arange(0, ROPE_DIM, 2, dtype=F32) / ROPE_DIM)
    ang = positions.reshape(-1).astype(F32)[:, None] * inv_freq
    cos, sin = jnp.cos(ang), jnp.sin(ang)
    n = ang.shape[0]
    one = jnp.ones((n, ATT_HEAD_DIM - ROPE_DIM), F32)
    zero8 = jnp.zeros((n, half), F32)
    zero = jnp.zeros((n, ATT_HEAD_DIM - ROPE_DIM), F32)
    cos64 = jnp.concatenate([cos, cos, one], axis=1)
    sa64 = jnp.concatenate([zero8, sin, zero], axis=1)
    sb64 = jnp.concatenate([-sin, zero8, zero], axis=1)
    dup = lambda t: jnp.concatenate([t, t], axis=1)
    return dup(cos64), dup(sa64), dup(sb64)


def _attn_body(lam_ref, g_ref, q_ref, k_ref, v_ref, o_ref, qs_ref, m_ref, acc_ref, s_ref,
               *, tk, lam_init):
    tq = q_ref.shape[0]
    s_len = k_ref.shape[0]
    lane = lax.broadcasted_iota(jnp.int32, (1, LANES), 1)
    first = lane < ATT_HEAD_DIM
    q = q_ref[...]
    zero = jnp.zeros_like(q)
    qs_ref[0:tq, :] = jnp.where(first, q, zero)
    qs_ref[tq:2 * tq, :] = jnp.where(first, zero, q)
    m_ref[...] = jnp.full(m_ref.shape, NEG_INF, F32)
    acc_ref[...] = jnp.zeros(acc_ref.shape, F32)
    ones = jnp.ones((tk, LANES), BF16)

    def scores(kc, slot):
        off = pl.multiple_of(kc * tk, tk)
        s_ref[slot] = lax.dot_general(qs_ref[...], k_ref[pl.ds(off, tk), :],
                                      (((1,), (1,)), ((), ())),
                                      preferred_element_type=F32)

    def update(kc, slot):
        off = pl.multiple_of(kc * tk, tk)
        v_ext = jnp.concatenate([v_ref[pl.ds(off, tk), :], ones], axis=1)
        s = s_ref[slot]
        m_old = m_ref[...]
        m_new = jnp.maximum(m_old, jnp.max(s, axis=-1, keepdims=True))
        alpha = jnp.exp2(m_old - m_new)
        p = jnp.concatenate(
            [jnp.exp2(s[:, j * LANES:(j + 1) * LANES] - m_new) for j in range(tk // LANES)], axis=1)
        pv = jnp.dot(p.astype(BF16), v_ext, preferred_element_type=F32)
        acc_ref[...] = jnp.concatenate([alpha, alpha], axis=1) * acc_ref[...] + pv
        m_ref[...] = m_new

    n_chunks = s_len // tk
    assert n_chunks % 2 == 0
    scores(0, 0)

    def step(i, carry):
        scores(2 * i + 1, 1)
        update(2 * i, 0)
        scores(jnp.minimum(2 * i + 2, n_chunks - 1), 0)
        update(2 * i + 1, 1)
        return carry

    lax.fori_loop(0, n_chunks // 2, step, 0)

    lp = lam_ref[...].astype(F32)
    lam = (jnp.exp(jnp.sum(lp[0:1] * lp[1:2], axis=-1, keepdims=True))
           - jnp.exp(jnp.sum(lp[2:3] * lp[3:4], axis=-1, keepdims=True)) + lam_init)
    a1 = acc_ref[0:tq, :]
    a2 = acc_ref[tq:2 * tq, :]
    out = a1[:, :LANES] / a1[:, LANES:] - lam * (a2[:, :LANES] / a2[:, LANES:])
    ms = jnp.mean(out * out, axis=-1, keepdims=True)
    o_ref[...] = (out * lax.rsqrt(ms + RMS_EPS) * g_ref[...] * (1.0 - lam_init)).astype(o_ref.dtype)


def diff_attention(proj, v_blk, lam_params, subln_g, lam_init, batch, seq, tq=1024, tk=512):
    n = batch * seq
    tq = min(tq, seq)
    tk = min(tk, seq)
    nq = seq // tq
    h = ATT_HEADS
    return pl.pallas_call(
        functools.partial(_attn_body, tk=tk, lam_init=lam_init),
        grid=(batch, h, nq),
        in_specs=[pl.BlockSpec((4, ATT_HEAD_DIM), lambda b, hh, i: (0, 0)),
                  pl.BlockSpec((1, LANES), lambda b, hh, i: (0, 0)),
                  pl.BlockSpec((tq, LANES), lambda b, hh, i: (b * nq + i, hh)),
                  pl.BlockSpec((seq, LANES), lambda b, hh, i: (b, h + hh)),
                  pl.BlockSpec((seq, LANES), lambda b, hh, i: (b, v_blk + hh))],
        out_specs=pl.BlockSpec((tq, LANES), lambda b, hh, i: (b * nq + i, hh)),
        out_shape=jax.ShapeDtypeStruct((n, h * LANES), BF16),
        scratch_shapes=[pltpu.VMEM((2 * tq, LANES), BF16), pltpu.VMEM((2 * tq, LANES), F32),
                        pltpu.VMEM((2 * tq, 2 * LANES), F32),
                        pltpu.VMEM((2, 2 * tq, tk), F32)],
        compiler_params=_params("parallel", "parallel", "parallel"),
        name="diff_attention",
    )(lam_params.astype(F32), subln_g.reshape(1, LANES).astype(F32), proj, proj, proj)


HALO = 8


def _conv_body(prev_ref, cur_ref, next_ref, w_ref, b_ref, o_ref, *, n_seq_blocks):
    i = pl.program_id(1)
    ts = cur_ref.shape[0]
    prev = jnp.where(i > 0, prev_ref[...].astype(F32), 0.0)
    nxt = jnp.where(i < n_seq_blocks - 1, next_ref[...].astype(F32), 0.0)
    ext = jnp.concatenate([prev, cur_ref[...].astype(F32), nxt], axis=0)
    w = w_ref[...]
    acc = jnp.zeros(cur_ref.shape, F32) + b_ref[...]
    pad = SSM_CONV // 2
    for kk in range(SSM_CONV):
        start = HALO - pad + kk
        acc = acc + ext[start:start + ts, :] * w[kk:kk + 1, :]
    o_ref[...] = (acc * jax.nn.sigmoid(acc)).astype(o_ref.dtype)


def conv_silu(proj, col_off, conv_w, conv_b, batch, seq, ts=512, tc=512):
    n = batch * seq
    c = conv_w.shape[1]
    ts = min(ts, seq)
    nsb = seq // ts
    assert col_off % tc == 0 and c % tc == 0 and ts % HALO == 0
    cb = col_off // tc
    hb = ts // HALO
    last_halo = n // HALO - 1

    def prev_map(b, i, j):
        return (jnp.maximum((b * nsb + i) * hb - 1, 0), cb + j)

    def next_map(b, i, j):
        return (jnp.minimum((b * nsb + i + 1) * hb, last_halo), cb + j)

    return pl.pallas_call(
        functools.partial(_conv_body, n_seq_blocks=nsb),
        grid=(batch, nsb, c // tc),
        in_specs=[pl.BlockSpec((HALO, tc), prev_map),
                  pl.BlockSpec((ts, tc), lambda b, i, j: (b * nsb + i, cb + j)),
                  pl.BlockSpec((HALO, tc), next_map),
                  pl.BlockSpec((SSM_CONV, tc), lambda b, i, j: (0, j)),
                  pl.BlockSpec((1, tc), lambda b, i, j: (0, j))],
        out_specs=pl.BlockSpec((ts, tc), lambda b, i, j: (b * nsb + i, j)),
        out_shape=jax.ShapeDtypeStruct((n, c), BF16),
        compiler_params=_params("parallel", "parallel", "parallel"),
        name="conv_silu",
    )(proj, proj, proj, conv_w.astype(F32), conv_b.reshape(1, c).astype(F32))


def _softplus(x):
    return jnp.maximum(x, 0.0) + jnp.log1p(jnp.exp(-jnp.abs(x)))


def _ssd_body(x_ref, b_ref, c_ref, dt_ref, dtt_ref, bias_ref, biast_ref, alog_ref, alogt_ref,
              y_ref, state_ref, *, reverse):
    q = SSM_CHUNK
    hpg = SSM_HEADS // SSM_GROUPS
    gw = hpg * SSM_HEAD_DIM
    hi = lax.Precision.HIGHEST

    @pl.when(pl.program_id(1) == 0)
    def _():
        state_ref[...] = jnp.zeros(state_ref.shape, F32)

    row = lax.broadcasted_iota(jnp.int32, (q, q), 0)
    col = lax.broadcasted_iota(jnp.int32, (q, q), 1)
    keep = (col >= row) if reverse else (col <= row)
    incl = keep.astype(BF16)
    incl_t = ((row >= col) if reverse else (row <= col)).astype(BF16)

    def split(v):
        v_hi = v.astype(BF16)
        return v_hi, (v - v_hi.astype(F32)).astype(BF16)

    dt = _softplus(dt_ref[...] + bias_ref[...])
    dtt = _softplus(dtt_ref[...] + biast_ref[...])
    a = dt * (-jnp.exp(alog_ref[...]))
    at = dtt * (-jnp.exp(alogt_ref[...]))
    a_hi, a_lo = split(a)
    at_hi, at_lo = split(at)
    cum = (jnp.dot(incl, a_hi, preferred_element_type=F32)
           + jnp.dot(incl, a_lo, preferred_element_type=F32))
    cum_t = (jnp.dot(at_hi, incl_t, preferred_element_type=F32)
             + jnp.dot(at_lo, incl_t, preferred_element_type=F32))
    total = jnp.sum(a, axis=0, keepdims=True)

    hid = lax.broadcasted_iota(jnp.int32, (SSM_HEADS, SSM_HEADS * SSM_HEAD_DIM), 0)
    lid = lax.broadcasted_iota(jnp.int32, (SSM_HEADS, SSM_HEADS * SSM_HEAD_DIM), 1)
    expand = (lid // SSM_HEAD_DIM == hid).astype(BF16)

    def ex(v):
        v_hi, v_lo = split(v)
        return (jnp.dot(v_hi, expand, preferred_element_type=F32)
                + jnp.dot(v_lo, expand, preferred_element_type=F32))

    xdec = (x_ref[...].astype(F32) * ex(dt * jnp.exp(total - cum))).astype(BF16)
    dec_out = ex(jnp.exp(cum))
    dec_chunk = ex(jnp.exp(total))

    lane_g = lax.broadcasted_iota(jnp.int32, (1, gw), 1) // SSM_HEAD_DIM

    for g in range(SSM_GROUPS):
        bg = b_ref[:, g * SSM_STATE:(g + 1) * SSM_STATE]
        cg = c_ref[:, g * SSM_STATE:(g + 1) * SSM_STATE]
        cb = lax.dot_general(cg, bg, (((1,), (1,)), ((), ())), preferred_element_type=F32)
        xg = x_ref[:, g * gw:(g + 1) * gw]
        ms, xbd = [], []
        for hh in range(hpg):
            head = g * hpg + hh
            diff = cum[:, head:head + 1] - cum_t[head:head + 1, :]
            decay = jnp.exp(jnp.where(keep, diff, -1e30)) * dtt[head:head + 1, :]
            ms.append((cb * decay).astype(BF16))
            xbd.append(jnp.where(lane_g == hh, xg, jnp.zeros_like(xg)))
        m_cat = jnp.concatenate(ms, axis=1)
        x_bd = jnp.concatenate(xbd, axis=0)
        y_diag = jnp.dot(m_cat, x_bd, preferred_element_type=F32)

        st = state_ref[g]
        y_off = jnp.dot(cg, st.astype(BF16), preferred_element_type=F32) * dec_out[:, g * gw:(g + 1) * gw]
        y_ref[:, g * gw:(g + 1) * gw] = (y_diag + y_off).astype(y_ref.dtype)

        new = lax.dot_general(bg, xdec[:, g * gw:(g + 1) * gw], (((0,), (0,)), ((), ())),
                              preferred_element_type=F32)
        state_ref[g] = st * dec_chunk[:, g * gw:(g + 1) * gw] + new


def ssd_scan(xc, dt_raw, dt_bias, a_log, batch, seq, reverse):
    n = batch * seq
    q = SSM_CHUNK
    nc = seq // q
    hp = SSM_HEADS * SSM_HEAD_DIM
    gn = SSM_GROUPS * SSM_STATE
    assert xc.shape[1] == hp + 2 * gn and hp == 2 * gn

    def cidx(c):
        return nc - 1 - c if reverse else c

    return pl.pallas_call(
        functools.partial(_ssd_body, reverse=reverse),
        grid=(batch, nc),
        in_specs=[pl.BlockSpec((q, hp), lambda b, c: (b * nc + cidx(c), 0)),
                  pl.BlockSpec((q, gn), lambda b, c: (b * nc + cidx(c), 2)),
                  pl.BlockSpec((q, gn), lambda b, c: (b * nc + cidx(c), 3)),
                  pl.BlockSpec((q, SSM_HEADS), lambda b, c: (b * nc + cidx(c), 0)),
                  pl.BlockSpec((SSM_HEADS, q), lambda b, c: (0, b * nc + cidx(c))),
                  pl.BlockSpec((1, SSM_HEADS), lambda b, c: (0, 0)),
                  pl.BlockSpec((SSM_HEADS, 1), lambda b, c: (0, 0)),
                  pl.BlockSpec((1, SSM_HEADS), lambda b, c: (0, 0)),
                  pl.BlockSpec((SSM_HEADS, 1), lambda b, c: (0, 0))],
        out_specs=pl.BlockSpec((q, hp), lambda b, c: (b * nc + cidx(c), 0)),
        out_shape=jax.ShapeDtypeStruct((n, hp), BF16),
        scratch_shapes=[pltpu.VMEM((SSM_GROUPS, SSM_STATE, hp // SSM_GROUPS), F32)],
        compiler_params=_params("parallel", "arbitrary"),
        name="ssd_scan_bwd" if reverse else "ssd_scan_fwd",
    )(xc, xc, xc, dt_raw, dt_raw.T, dt_bias.reshape(1, -1).astype(F32),
      dt_bias.reshape(-1, 1).astype(F32), a_log.reshape(1, -1).astype(F32),
      a_log.reshape(-1, 1).astype(F32))


def _gated_norm_body(yf_ref, yb_ref, xs_ref, z_ref, d_ref, g_ref, o_ref):
    z = z_ref[...].astype(F32)
    y = (yf_ref[...].astype(F32) + yb_ref[...].astype(F32)
         + d_ref[...] * xs_ref[...].astype(F32)) * (z * jax.nn.sigmoid(z))
    ms = jnp.mean(y * y, axis=-1, keepdims=True)
    o_ref[...] = (y * lax.rsqrt(ms + RMS_EPS) * g_ref[...]).astype(o_ref.dtype)


def gated_norm(y_f, y_b, xc, proj, z_col_block, d_skip, norm_g, tm=256):
    n, c = y_f.shape
    tm = min(tm, n)
    row = lambda i: (i, 0)
    return pl.pallas_call(
        _gated_norm_body,
        grid=(n // tm,),
        in_specs=[pl.BlockSpec((tm, c), row), pl.BlockSpec((tm, c), row),
                  pl.BlockSpec((tm, c), row),
                  pl.BlockSpec((tm, c), lambda i: (i, z_col_block)),
                  pl.BlockSpec((1, c), lambda i: (0, 0)), pl.BlockSpec((1, c), lambda i: (0, 0))],
        out_specs=pl.BlockSpec((tm, c), row),
        out_shape=jax.ShapeDtypeStruct((n, c), BF16),
        compiler_params=_params("parallel"),
        name="gated_norm",
    )(y_f, y_b, xc, proj, jnp.repeat(d_skip.astype(F32), SSM_HEAD_DIM).reshape(1, c),
      norm_g.reshape(1, c).astype(F32))


def _merge_body(att_ref, m_ref, wa_ref, ws_ref, ga_ref, gs_ref, ba_ref, bs_ref, o_ref):
    a_out = jnp.dot(att_ref[...], wa_ref[...], preferred_element_type=F32)
    m_out = jnp.dot(m_ref[...], ws_ref[...], preferred_element_type=F32)
    g_att = jax.nn.sigmoid(ga_ref[...] + ba_ref[...])
    g_ssm = jax.nn.sigmoid(gs_ref[...] + bs_ref[...])
    o_ref[...] = (g_att * a_out + g_ssm * m_out).astype(o_ref.dtype)


def branch_merge(att, m, w_att, w_ssm, gate_logits, gate_b, tm=512, tn=512):
    n, d = att.shape[0], w_att.shape[1]
    tm = min(tm, n)
    nb = d // tn
    return pl.pallas_call(
        _merge_body,
        grid=(nb, n // tm),
        in_specs=[pl.BlockSpec((tm, att.shape[1]), lambda j, i: (i, 0)),
                  pl.BlockSpec((tm, m.shape[1]), lambda j, i: (i, 0)),
                  pl.BlockSpec((w_att.shape[0], tn), lambda j, i: (0, j)),
                  pl.BlockSpec((w_ssm.shape[0], tn), lambda j, i: (0, j)),
                  pl.BlockSpec((tm, tn), lambda j, i: (i, j)),
                  pl.BlockSpec((tm, tn), lambda j, i: (i, nb + j)),
                  pl.BlockSpec((1, tn), lambda j, i: (0, j)),
                  pl.BlockSpec((1, tn), lambda j, i: (0, nb + j))],
        out_specs=pl.BlockSpec((tm, tn), lambda j, i: (i, j)),
        out_shape=jax.ShapeDtypeStruct((n, d), BF16),
        compiler_params=_params("parallel", "parallel"),
        name="branch_merge",
    )(att, m, w_att, w_ssm, gate_logits, gate_logits, gate_b.reshape(1, -1).astype(F32),
      gate_b.reshape(1, -1).astype(F32))


def _peer_score_body(h_ref, wq_ref, keys_ref, o_ref):
    qry = jnp.dot(h_ref[...], wq_ref[...], preferred_element_type=F32).astype(BF16)
    for hc in range(2 * PEER_HEADS):
        qh = qry[:, hc * LANES:(hc + 1) * LANES]
        o_ref[hc] = lax.dot_general(keys_ref[hc], qh, (((1,), (1,)), ((), ())),
                                    preferred_element_type=F32)


def peer_scores(h, wq, keys, tm=512):
    n, d = h.shape
    tm = min(tm, n)
    nhc = 2 * PEER_HEADS
    return pl.pallas_call(
        _peer_score_body,
        grid=(n // tm,),
        in_specs=[pl.BlockSpec((tm, d), lambda i: (i, 0)),
                  pl.BlockSpec(wq.shape, lambda i: (0, 0)),
                  pl.BlockSpec(keys.shape, lambda i: (0, 0, 0))],
        out_specs=pl.BlockSpec((nhc, PEER_KEYS, tm), lambda i: (0, 0, i)),
        out_shape=jax.ShapeDtypeStruct((nhc, PEER_KEYS, n), F32),
        compiler_params=_params("parallel"),
        name="peer_scores",
    )(h, wq, keys)


def _top16_rows(x):
    r = x.shape[0]
    rid = lax.broadcasted_iota(jnp.int32, x.shape, 0)
    out = []
    for _ in range(PEER_TOPK):
        m = jnp.max(x, axis=0, keepdims=True)
        first = jnp.min(jnp.where(x == m, rid, r), axis=0, keepdims=True)
        x = jnp.where(rid == first, NEG_INF, x)
        out.append(m)
    return out


def _batcher_pairs(n):
    pairs, p = [], 1
    while p < n:
        k = p
        while k >= 1:
            for j in range(k % p, n - k, 2 * k):
                for i in range(min(k, n - j - k)):
                    if (i + j) // (2 * p) == (i + j + k) // (2 * p):
                        pairs.append((i + j, i + j + k))
            k //= 2
        p *= 2
    return pairs


def _top16_sorted(x_ref, idx):
    n = PEER_TOPK
    v = [x_ref[idx, 8 * k:8 * k + 8, :] for k in range(n)]

    def cmpx(i, j):
        v[i], v[j] = jnp.maximum(v[i], v[j]), jnp.minimum(v[i], v[j])

    for i, j in _batcher_pairs(n):
        cmpx(i, j)
    for shift in (4, 2, 1):
        other = [pltpu.roll(vk, shift, axis=0) for vk in v]
        v = [jnp.maximum(v[k], other[n - 1 - k]) for k in range(n)]
        d = n // 2
        while d >= 1:
            for k in range(n):
                if k & d == 0:
                    cmpx(k, k + d)
            d //= 2
    return v


def _peer_route_body(sc_ref, st_ref):
    t = sc_ref.shape[2]
    sub = lax.broadcasted_iota(jnp.int32, (8, t), 0)
    for h in range(PEER_HEADS):
        va = _top16_sorted(sc_ref, 2 * h)
        vb = _top16_sorted(sc_ref, 2 * h + 1)

        def stack8(rows):
            acc = jnp.zeros((8, t), F32)
            for i, rw in enumerate(rows):
                acc = jnp.where(sub == i, rw, acc)
            return acc

        va_lo, va_hi = stack8(va[:8]), stack8(va[8:])
        vb_hi = stack8(vb[8:])
        groups = [va_lo + vb[0], va_hi + vb[0], va_lo + vb[1]]
        for qq, lim in ((2, 5), (3, 4), (4, 3), (5, 2), (6, 2), (7, 2)):
            groups.append(jnp.where(sub < lim, va_lo + vb[qq], NEG_INF))
        groups.append(va[0] + vb_hi)
        cand = jnp.concatenate(groups, axis=0)
        best = _top16_rows(cand)
        z = jnp.zeros((1, t), F32)
        for bv in best:
            z = z + jnp.exp(bv - best[0])
        st_ref[0, h:h + 1, :] = best[PEER_TOPK - 1]
        st_ref[1, h:h + 1, :] = va[0][0:1, :]
        st_ref[2, h:h + 1, :] = vb[0][0:1, :]
        st_ref[3, h:h + 1, :] = z


def peer_route(sc_t, tt=256):
    nhc, kk, n = sc_t.shape
    tt = min(tt, n)
    return pl.pallas_call(
        _peer_route_body,
        grid=(n // tt,),
        in_specs=[pl.BlockSpec((nhc, kk, tt), lambda i: (0, 0, i))],
        out_specs=pl.BlockSpec((4, PEER_HEADS, tt), lambda i: (0, 0, i)),
        out_shape=jax.ShapeDtypeStruct((4, PEER_HEADS, n), F32),
        compiler_params=_params("parallel"),
        name="peer_route",
    )(sc_t)


PEER_PAIR = 2 * PEER_KEYS


def _peer_mix_body(ht_ref, u_ref, vt_ref, sc_ref, st_ref, x_ref, o_ref,
                   acc_ref, ea_ref, eb_ref, s0_ref, s1_ref, w0_ref, w1_ref, *, tb):
    c = pl.program_id(1)
    npair = u_ref.shape[0]
    tm = ht_ref.shape[1]
    ipp = PEER_PAIR // PEER_KEYS
    s_slots = (s0_ref, s1_ref)
    w_slots = (w0_ref, w1_ref)

    @pl.when(c == 0)
    def _():
        acc_ref[...] = jnp.zeros(acc_ref.shape, F32)
        for h in range(PEER_HEADS):
            half_inv_z = 0.5 / st_ref[3, h:h + 1, :]
            ea_ref[h] = jnp.exp(sc_ref[2 * h] - st_ref[1, h:h + 1, :]) * half_inv_z
            eb_ref[h] = jnp.exp(sc_ref[2 * h + 1] - st_ref[2, h:h + 1, :])

    def score(pair, slot):
        s_slots[slot][...] = jnp.dot(u_ref[pair], ht_ref[...],
                                     preferred_element_type=F32)

    def gate_act(pair, slot):
        for ii in range(ipp):
            i = (c * npair + pair) * ipp + ii
            a_rows = [sc_ref[2 * h, pl.ds(i, 1), :] for h in range(PEER_HEADS)]
            ea_rows = [ea_ref[h, pl.ds(i, 1), :] for h in range(PEER_HEADS)]
            rows = slice(ii * PEER_KEYS, (ii + 1) * PEER_KEYS)
            for t_i in range(tm // tb):
                ln = slice(t_i * tb, (t_i + 1) * tb)
                gate = jnp.zeros((PEER_KEYS, tb), F32)
                for h in range(PEER_HEADS):
                    tau = st_ref[0, h:h + 1, ln]
                    ssum = a_rows[h][:, ln] + sc_ref[2 * h + 1, :, ln]
                    gate = gate + jnp.where(ssum >= tau, ea_rows[h][:, ln] * eb_ref[h, :, ln], 0.0)
                s_blk = s_slots[slot][rows, ln]
                act = s_blk * (1.0 + lax.erf(s_blk * (2.0 ** -0.5)))
                w_slots[slot][rows, ln] = (act * gate).astype(BF16)

    def mix(pair, slot):
        acc_ref[...] += jnp.dot(vt_ref[pair], w_slots[slot][...],
                                preferred_element_type=F32)

    score(0, 0)

    def trip(q, carry):
        score(2 * q + 1, 1)
        gate_act(2 * q, 0)
        mix(2 * q, 0)
        score(jnp.minimum(2 * q + 2, npair - 1), 0)
        gate_act(2 * q + 1, 1)
        mix(2 * q + 1, 1)
        return carry

    lax.fori_loop(0, npair // 2, trip, 0)

    @pl.when(c == pl.num_programs(1) - 1)
    def _():
        o_ref[...] = x_ref[...] + acc_ref[...].T


def peer_mix(h_t, u3, vt3, sc_t, stats, x_res, tm=256, npair=8, tb=128):
    d, n = h_t.shape
    tm = min(tm, n)
    tb = min(tb, tm)
    nhc = 2 * PEER_HEADS
    assert u3.shape[0] % npair == 0 and npair % 2 == 0 and tm % tb == 0
    return pl.pallas_call(
        functools.partial(_peer_mix_body, tb=tb),
        grid=(n // tm, u3.shape[0] // npair),
        in_specs=[pl.BlockSpec((d, tm), lambda t, c: (0, t)),
                  pl.BlockSpec((npair, PEER_PAIR, d), lambda t, c: (c, 0, 0)),
                  pl.BlockSpec((npair, d, PEER_PAIR), lambda t, c: (c, 0, 0)),
                  pl.BlockSpec((nhc, PEER_KEYS, tm), lambda t, c: (0, 0, t)),
                  pl.BlockSpec((4, PEER_HEADS, tm), lambda t, c: (0, 0, t)),
                  pl.BlockSpec((tm, d), lambda t, c: (t, 0))],
        out_specs=pl.BlockSpec((tm, d), lambda t, c: (t, 0)),
        out_shape=jax.ShapeDtypeStruct((n, d), F32),
        scratch_shapes=[pltpu.VMEM((d, tm), F32),
                        pltpu.VMEM((PEER_HEADS, PEER_KEYS, tm), F32),
                        pltpu.VMEM((PEER_HEADS, PEER_KEYS, tm), F32),
                        pltpu.VMEM((PEER_PAIR, tm), F32), pltpu.VMEM((PEER_PAIR, tm), F32),
                        pltpu.VMEM((PEER_PAIR, tm), BF16), pltpu.VMEM((PEER_PAIR, tm), BF16)],
        compiler_params=_params("parallel", "arbitrary"),
        name="peer_mix",
    )(h_t, u3, vt3, sc_t, stats, x_res)


def kernel(x, positions, norm1_g, w_in, gate_b, diff_lam, subln_g, w_att_br, conv_w, conv_b,
           dt_bias, a_log, d_skip, ssm_norm_g, w_ssm_br, w_out, norm2_g, peer_wq, peer_keys,
           peer_u, peer_v, final_g):
    batch, seq, d = x.shape
    depth = w_in.shape[0]
    n = batch * seq
    qk_cols = 2 * ATT_HEADS * 2 * ATT_HEAD_DIM
    qkv_cols = 3 * ATT_HEADS * 2 * ATT_HEAD_DIM
    z_cols = SSM_HEADS * SSM_HEAD_DIM
    xbc_cols = z_cols + 2 * SSM_GROUPS * SSM_STATE
    main_cols = qkv_cols + z_cols + xbc_cols
    dt_cols = 2 * SSM_HEADS

    cosf, sin_a, sin_b = rotary_tables(positions)
    q_scale = ATT_HEAD_DIM ** -0.5 * LOG2E
    xf = x.reshape(n, d)

    for l in range(depth):
        lam_init = 0.8 - 0.6 * math.exp(-0.3 * l)
        w_l = w_in[l]
        w_main = jnp.concatenate([w_l[:, :qk_cols], w_l[:, qkv_cols:main_cols],
                                  w_l[:, qk_cols:qkv_cols]], axis=1).astype(BF16)
        w_dt = w_l[:, main_cols:main_cols + dt_cols].astype(BF16)
        w_gate = w_l[:, main_cols + dt_cols:].astype(BF16)

        h1 = rmsnorm(xf, norm1_g[l])
        proj = inproj_rotary(h1, w_main, cosf, sin_a, sin_b, q_scale)
        dt_raw = matmul(h1, w_dt, F32)
        gate_logits = matmul(h1, w_gate, F32)

        att = diff_attention(proj, (qk_cols + z_cols + xbc_cols) // LANES, diff_lam[l],
                             subln_g[l], lam_init, batch, seq)

        xc = conv_silu(proj, qk_cols + z_cols, conv_w[l], conv_b[l], batch, seq)
        y_f = ssd_scan(xc, dt_raw[:, :SSM_HEADS], dt_bias[l, 0], a_log[l, 0], batch, seq, False)
        y_b = ssd_scan(xc, dt_raw[:, SSM_HEADS:], dt_bias[l, 1], a_log[l, 1], batch, seq, True)
        m = gated_norm(y_f, y_b, xc, proj, qk_cols // z_cols, d_skip[l], ssm_norm_g[l])

        merged = branch_merge(att, m, w_att_br[l].astype(BF16), w_ssm_br[l].astype(BF16),
                              gate_logits, gate_b[l])
        xf = matmul(merged, w_out[l].astype(BF16), F32, residual=xf)

        h2 = rmsnorm(xf, norm2_g[l])
        keys = peer_keys[l].reshape(2 * PEER_HEADS, PEER_KEYS, -1).astype(BF16)
        sc_t = peer_scores(h2, peer_wq[l].astype(BF16), keys)
        stats = peer_route(sc_t)
        u3 = peer_u[l].astype(BF16).reshape(-1, PEER_PAIR, d)
        vt3 = peer_v[l].astype(BF16).reshape(-1, PEER_PAIR, d).transpose(0, 2, 1)
        xf = peer_mix(h2.T, u3, vt3, sc_t, stats, xf)

    return rmsnorm(xf, final_g, out_dtype=x.dtype).reshape(batch, seq, d)
```

```python
import functools
import math

import jax
import jax.numpy as jnp
from jax import lax
from jax.experimental import pallas as pl
from jax.experimental.pallas import tpu as pltpu

F32 = jnp.float32
BF16 = jnp.bfloat16

LANES = 128
VMEM_LIMIT = 48 * 1024 * 1024

ATT_HEADS = 8
ATT_HEAD_DIM = 64
ROPE_DIM = 16
ROPE_THETA = 500000.0
SSM_HEADS = 32
SSM_HEAD_DIM = 64
SSM_GROUPS = 8
SSM_STATE = 128
SSM_CONV = 5
SSM_CHUNK = 128
PEER_HEADS = 8
PEER_KEYS = 128
PEER_TOPK = 16
RMS_EPS = 1e-6
NEG_INF = float("-inf")
LOG2E = 1.4426950408889634


def _params(*sem, flags=None):
    return pltpu.CompilerParams(dimension_semantics=sem, vmem_limit_bytes=VMEM_LIMIT, flags=flags)


def _rmsnorm_body(x_ref, g_ref, o_ref):
    x = x_ref[...].astype(F32)
    ms = jnp.mean(x * x, axis=-1, keepdims=True)
    o_ref[...] = (x * lax.rsqrt(ms + RMS_EPS) * g_ref[...]).astype(o_ref.dtype)


def rmsnorm(x, g, out_dtype=BF16, tm=512):
    n, d = x.shape
    tm = min(tm, n)
    return pl.pallas_call(
        _rmsnorm_body,
        grid=(n // tm,),
        in_specs=[pl.BlockSpec((tm, d), lambda i: (i, 0)),
                  pl.BlockSpec((1, d), lambda i: (0, 0))],
        out_specs=pl.BlockSpec((tm, d), lambda i: (i, 0)),
        out_shape=jax.ShapeDtypeStruct((n, d), out_dtype),
        compiler_params=_params("parallel"),
        name="rmsnorm",
    )(x, g.reshape(1, d).astype(F32))


def _mm_body(a_ref, w_ref, o_ref):
    o_ref[...] = jnp.dot(a_ref[...], w_ref[...], preferred_element_type=F32).astype(o_ref.dtype)


def _mm_res_body(a_ref, w_ref, r_ref, o_ref):
    acc = jnp.dot(a_ref[...], w_ref[...], preferred_element_type=F32)
    o_ref[...] = (r_ref[...].astype(F32) + acc).astype(o_ref.dtype)


def matmul(a, w, out_dtype, residual=None, tm=512, tn=1024):
    m, k = a.shape
    n = w.shape[1]
    tm = min(tm, m)
    tn = min(tn, n)
    assert m % tm == 0 and n % tn == 0
    in_specs = [pl.BlockSpec((tm, k), lambda j, i: (i, 0)),
                pl.BlockSpec((k, tn), lambda j, i: (0, j))]
    args = [a, w]
    body = _mm_body
    if residual is not None:
        in_specs.append(pl.BlockSpec((tm, tn), lambda j, i: (i, j)))
        args.append(residual)
        body = _mm_res_body
    return pl.pallas_call(
        body,
        grid=(n // tn, m // tm),
        in_specs=in_specs,
        out_specs=pl.BlockSpec((tm, tn), lambda j, i: (i, j)),
        out_shape=jax.ShapeDtypeStruct((m, n), out_dtype),
        compiler_params=_params("parallel", "parallel"),
        name="matmul",
    )(*args)


def _inproj_body(a_ref, w_ref, cos_ref, sa_ref, sb_ref, o_ref, *, q_scale):
    j = pl.program_id(0)
    acc = jnp.dot(a_ref[...], w_ref[...], preferred_element_type=F32)

    @pl.when(j >= 2)
    def _():
        o_ref[...] = acc.astype(o_ref.dtype)

    @pl.when(j < 2)
    def _():
        scale = jnp.where(j == 0, q_scale, 1.0).astype(F32)
        cosf = cos_ref[...] * scale
        sa = sa_ref[...] * scale
        sb = sb_ref[...] * scale
        tn = acc.shape[1]
        for g in range(tn // LANES):
            t = acc[:, g * LANES:(g + 1) * LANES]
            half = ROPE_DIM // 2
            r = (t * cosf + pltpu.roll(t, half, axis=1) * sa
                 + pltpu.roll(t, LANES - half, axis=1) * sb)
            o_ref[:, g * LANES:(g + 1) * LANES] = r.astype(o_ref.dtype)


def inproj_rotary(h, w, cosf, sin_a, sin_b, q_scale, tm=512, tn=1024):
    m, k = h.shape
    n = w.shape[1]
    tm = min(tm, m)
    assert m % tm == 0 and n % tn == 0
    tab = pl.BlockSpec((tm, LANES), lambda j, i: (i, 0))
    return pl.pallas_call(
        functools.partial(_inproj_body, q_scale=q_scale),
        grid=(n // tn, m // tm),
        in_specs=[pl.BlockSpec((tm, k), lambda j, i: (i, 0)),
                  pl.BlockSpec((k, tn), lambda j, i: (0, j)),
                  tab, tab, tab],
        out_specs=pl.BlockSpec((tm, tn), lambda j, i: (i, j)),
        out_shape=jax.ShapeDtypeStruct((m, n), BF16),
        compiler_params=_params("parallel", "parallel"),
        name="inproj_rotary",
    )(h, w, cosf, sin_a, sin_b)


def rotary_tables(positions):
    half = ROPE_DIM // 2
    inv_freq = ROPE_THETA ** (-jnp.arange(0, ROPE_DIM, 2, dtype=F32) / ROPE_DIM)
    ang = positions.reshape(-1).astype(F32)[:, None] * inv_freq
    cos, sin = jnp.cos(ang), jnp.sin(ang)
    n = ang.shape[0]
    one = jnp.ones((n, ATT_HEAD_DIM - ROPE_DIM), F32)
    zero8 = jnp.zeros((n, half), F32)
    zero = jnp.zeros((n, ATT_HEAD_DIM - ROPE_DIM), F32)
    cos64 = jnp.concatenate([cos, cos, one], axis=1)
    sa64 = jnp.concatenate([zero8, sin, zero], axis=1)
    sb64 = jnp.concatenate([-sin, zero8, zero], axis=1)
    dup = lambda t: jnp.concatenate([t, t], axis=1)
    return dup(cos64), dup(sa64), dup(sb64)


def _attn_body(lam_ref, g_ref, q_ref, k_ref, v_ref, o_ref, qs_ref, m_ref, acc_ref, s_ref,
               *, tk, lam_init):
    tq = q_ref.shape[0]
    s_len = k_ref.shape[0]
    lane = lax.broadcasted_iota(jnp.int32, (1, LANES), 1)
    first = lane < ATT_HEAD_DIM
    q = q_ref[...]
    zero = jnp.zeros_like(q)
    qs_ref[0:tq, :] = jnp.where(first, q, zero)
    qs_ref[tq:2 * tq, :] = jnp.where(first, zero, q)
    m_ref[...] = jnp.full(m_ref.shape, NEG_INF, F32)
    acc_ref[...] = jnp.zeros(acc_ref.shape, F32)
    ones = jnp.ones((tk, LANES), BF16)

    def scores(kc, slot):
        off = pl.multiple_of(kc * tk, tk)
        s_ref[slot] = lax.dot_general(qs_ref[...], k_ref[pl.ds(off, tk), :],
                                      (((1,), (1,)), ((), ())),
                                      preferred_element_type=F32)

    def update(kc, slot):
        off = pl.multiple_of(kc * tk, tk)
        v_ext = jnp.concatenate([v_ref[pl.ds(off, tk), :], ones], axis=1)
        s = s_ref[slot]
        m_old = m_ref[...]
        m_new = jnp.maximum(m_old, jnp.max(s, axis=-1, keepdims=True))
        alpha = jnp.exp2(m_old - m_new)
        p = jnp.concatenate(
            [jnp.exp2(s[:, j * LANES:(j + 1) * LANES] - m_new) for j in range(tk // LANES)], axis=1)
        pv = jnp.dot(p.astype(BF16), v_ext, preferred_element_type=F32)
        acc_ref[...] = jnp.concatenate([alpha, alpha], axis=1) * acc_ref[...] + pv
        m_ref[...] = m_new

    n_chunks = s_len // tk
    assert n_chunks % 2 == 0
    scores(0, 0)

    def step(i, carry):
        scores(2 * i + 1, 1)
        update(2 * i, 0)
        scores(jnp.minimum(2 * i + 2, n_chunks - 1), 0)
        update(2 * i + 1, 1)
        return carry

    lax.fori_loop(0, n_chunks // 2, step, 0)

    lp = lam_ref[...].astype(F32)
    lam = (jnp.exp(jnp.sum(lp[0:1] * lp[1:2], axis=-1, keepdims=True))
           - jnp.exp(jnp.sum(lp[2:3] * lp[3:4], axis=-1, keepdims=True)) + lam_init)
    a1 = acc_ref[0:tq, :]
    a2 = acc_ref[tq:2 * tq, :]
    out = a1[:, :LANES] / a1[:, LANES:] - lam * (a2[:, :LANES] / a2[:, LANES:])
    ms = jnp.mean(out * out, axis=-1, keepdims=True)
    o_ref[...] = (out * lax.rsqrt(ms + RMS_EPS) * g_ref[...] * (1.0 - lam_init)).astype(o_ref.dtype)


def diff_attention(proj, v_blk, lam_params, subln_g, lam_init, batch, seq, tq=1024, tk=512):
    n = batch * seq
    tq = min(tq, seq)
    tk = min(tk, seq)
    nq = seq // tq
    h = ATT_HEADS
    return pl.pallas_call(
        functools.partial(_attn_body, tk=tk, lam_init=lam_init),
        grid=(batch, h, nq),
        in_specs=[pl.BlockSpec((4, ATT_HEAD_DIM), lambda b, hh, i: (0, 0)),
                  pl.BlockSpec((1, LANES), lambda b, hh, i: (0, 0)),
                  pl.BlockSpec((tq, LANES), lambda b, hh, i: (b * nq + i, hh)),
                  pl.BlockSpec((seq, LANES), lambda b, hh, i: (b, h + hh)),
                  pl.BlockSpec((seq, LANES), lambda b, hh, i: (b, v_blk + hh))],
        out_specs=pl.BlockSpec((tq, LANES), lambda b, hh, i: (b * nq + i, hh)),
        out_shape=jax.ShapeDtypeStruct((n, h * LANES), BF16),
        scratch_shapes=[pltpu.VMEM((2 * tq, LANES), BF16), pltpu.VMEM((2 * tq, LANES), F32),
                        pltpu.VMEM((2 * tq, 2 * LANES), F32),
                        pltpu.VMEM((2, 2 * tq, tk), F32)],
        compiler_params=_params("parallel", "parallel", "parallel"),
        name="diff_attention",
    )(lam_params.astype(F32), subln_g.reshape(1, LANES).astype(F32), proj, proj, proj)


HALO = 8


def _conv_body(prev_ref, cur_ref, next_ref, w_ref, b_ref, o_ref, *, n_seq_blocks):
    i = pl.program_id(1)
    ts = cur_ref.shape[0]
    prev = jnp.where(i > 0, prev_ref[...].astype(F32), 0.0)
    nxt = jnp.where(i < n_seq_blocks - 1, next_ref[...].astype(F32), 0.0)
    ext = jnp.concatenate([prev, cur_ref[...].astype(F32), nxt], axis=0)
    w = w_ref[...]
    acc = jnp.zeros(cur_ref.shape, F32) + b_ref[...]
    pad = SSM_CONV // 2
    for kk in range(SSM_CONV):
        start = HALO - pad + kk
        acc = acc + ext[start:start + ts, :] * w[kk:kk + 1, :]
    o_ref[...] = (acc * jax.nn.sigmoid(acc)).astype(o_ref.dtype)


def conv_silu(proj, col_off, conv_w, conv_b, batch, seq, ts=512, tc=512):
    n = batch * seq
    c = conv_w.shape[1]
    ts = min(ts, seq)
    nsb = seq // ts
    assert col_off % tc == 0 and c % tc == 0 and ts % HALO == 0
    cb = col_off // tc
    hb = ts // HALO
    last_halo = n // HALO - 1

    def prev_map(b, i, j):
        return (jnp.maximum((b * nsb + i) * hb - 1, 0), cb + j)

    def next_map(b, i, j):
        return (jnp.minimum((b * nsb + i + 1) * hb, last_halo), cb + j)

    return pl.pallas_call(
        functools.partial(_conv_body, n_seq_blocks=nsb),
        grid=(batch, nsb, c // tc),
        in_specs=[pl.BlockSpec((HALO, tc), prev_map),
                  pl.BlockSpec((ts, tc), lambda b, i, j: (b * nsb + i, cb + j)),
                  pl.BlockSpec((HALO, tc), next_map),
                  pl.BlockSpec((SSM_CONV, tc), lambda b, i, j: (0, j)),
                  pl.BlockSpec((1, tc), lambda b, i, j: (0, j))],
        out_specs=pl.BlockSpec((ts, tc), lambda b, i, j: (b * nsb + i, j)),
        out_shape=jax.ShapeDtypeStruct((n, c), BF16),
        compiler_params=_params("parallel", "parallel", "parallel"),
        name="conv_silu",
    )(proj, proj, proj, conv_w.astype(F32), conv_b.reshape(1, c).astype(F32))


def _softplus(x):
    return jnp.maximum(x, 0.0) + jnp.log1p(jnp.exp(-jnp.abs(x)))


def _ssd_body(x_ref, b_ref, c_ref, dt_ref, dtt_ref, bias_ref, biast_ref, alog_ref, alogt_ref,
              y_ref, state_ref, *, reverse):
    q = SSM_CHUNK
    hpg = SSM_HEADS // SSM_GROUPS
    gw = hpg * SSM_HEAD_DIM
    hi = lax.Precision.HIGHEST

    @pl.when(pl.program_id(1) == 0)
    def _():
        state_ref[...] = jnp.zeros(state_ref.shape, F32)

    row = lax.broadcasted_iota(jnp.int32, (q, q), 0)
    col = lax.broadcasted_iota(jnp.int32, (q, q), 1)
    keep = (col >= row) if reverse else (col <= row)
    incl = keep.astype(BF16)
    incl_t = ((row >= col) if reverse else (row <= col)).astype(BF16)

    def split(v):
        v_hi = v.astype(BF16)
        return v_hi, (v - v_hi.astype(F32)).astype(BF16)

    dt = _softplus(dt_ref[...] + bias_ref[...])
    dtt = _softplus(dtt_ref[...] + biast_ref[...])
    a = dt * (-jnp.exp(alog_ref[...]))
    at = dtt * (-jnp.exp(alogt_ref[...]))
    a_hi, a_lo = split(a)
    at_hi, at_lo = split(at)
    cum = (jnp.dot(incl, a_hi, preferred_element_type=F32)
           + jnp.dot(incl, a_lo, preferred_element_type=F32))
    cum_t = (jnp.dot(at_hi, incl_t, preferred_element_type=F32)
             + jnp.dot(at_lo, incl_t, preferred_element_type=F32))
    total = jnp.sum(a, axis=0, keepdims=True)

    hid = lax.broadcasted_iota(jnp.int32, (SSM_HEADS, SSM_HEADS * SSM_HEAD_DIM), 0)
    lid = lax.broadcasted_iota(jnp.int32, (SSM_HEADS, SSM_HEADS * SSM_HEAD_DIM), 1)
    expand = (lid // SSM_HEAD_DIM == hid).astype(BF16)

    def ex(v):
        v_hi, v_lo = split(v)
        return (jnp.dot(v_hi, expand, preferred_element_type=F32)
                + jnp.dot(v_lo, expand, preferred_element_type=F32))

    xdec = (x_ref[...].astype(F32) * ex(dt * jnp.exp(total - cum))).astype(BF16)
    dec_out = ex(jnp.exp(cum))
    dec_chunk = ex(jnp.exp(total))

    lane_g = lax.broadcasted_iota(jnp.int32, (1, gw), 1) // SSM_HEAD_DIM

    for g in range(SSM_GROUPS):
        bg = b_ref[:, g * SSM_STATE:(g + 1) * SSM_STATE]
        cg = c_ref[:, g * SSM_STATE:(g + 1) * SSM_STATE]
        cb = lax.dot_general(cg, bg, (((1,), (1,)), ((), ())), preferred_element_type=F32)
        xg = x_ref[:, g * gw:(g + 1) * gw]
        ms, xbd = [], []
        for hh in range(hpg):
            head = g * hpg + hh
            diff = cum[:, head:head + 1] - cum_t[head:head + 1, :]
            decay = jnp.exp(jnp.where(keep, diff, -1e30)) * dtt[head:head + 1, :]
            ms.append((cb * decay).astype(BF16))
            xbd.append(jnp.where(lane_g == hh, xg, jnp.zeros_like(xg)))
        m_cat = jnp.concatenate(ms, axis=1)
        x_bd = jnp.concatenate(xbd, axis=0)
        y_diag = jnp.dot(m_cat, x_bd, preferred_element_type=F32)

        st = state_ref[g]
        y_off = jnp.dot(cg, st.astype(BF16), preferred_element_type=F32) * dec_out[:, g * gw:(g + 1) * gw]
        y_ref[:, g * gw:(g + 1) * gw] = (y_diag + y_off).astype(y_ref.dtype)

        new = lax.dot_general(bg, xdec[:, g * gw:(g + 1) * gw], (((0,), (0,)), ((), ())),
                              preferred_element_type=F32)
        state_ref[g] = st * dec_chunk[:, g * gw:(g + 1) * gw] + new


def ssd_scan(xc, dt_raw, dt_bias, a_log, batch, seq, reverse):
    n = batch * seq
    q = SSM_CHUNK
    nc = seq // q
    hp = SSM_HEADS * SSM_HEAD_DIM
    gn = SSM_GROUPS * SSM_STATE
    assert xc.shape[1] == hp + 2 * gn and hp == 2 * gn

    def cidx(c):
        return nc - 1 - c if reverse else c

    return pl.pallas_call(
        functools.partial(_ssd_body, reverse=reverse),
        grid=(batch, nc),
        in_specs=[pl.BlockSpec((q, hp), lambda b, c: (b * nc + cidx(c), 0)),
                  pl.BlockSpec((q, gn), lambda b, c: (b * nc + cidx(c), 2)),
                  pl.BlockSpec((q, gn), lambda b, c: (b * nc + cidx(c), 3)),
                  pl.BlockSpec((q, SSM_HEADS), lambda b, c: (b * nc + cidx(c), 0)),
                  pl.BlockSpec((SSM_HEADS, q), lambda b, c: (0, b * nc + cidx(c))),
                  pl.BlockSpec((1, SSM_HEADS), lambda b, c: (0, 0)),
                  pl.BlockSpec((SSM_HEADS, 1), lambda b, c: (0, 0)),
                  pl.BlockSpec((1, SSM_HEADS), lambda b, c: (0, 0)),
                  pl.BlockSpec((SSM_HEADS, 1), lambda b, c: (0, 0))],
        out_specs=pl.BlockSpec((q, hp), lambda b, c: (b * nc + cidx(c), 0)),
        out_shape=jax.ShapeDtypeStruct((n, hp), BF16),
        scratch_shapes=[pltpu.VMEM((SSM_GROUPS, SSM_STATE, hp // SSM_GROUPS), F32)],
        compiler_params=_params("parallel", "arbitrary"),
        name="ssd_scan_bwd" if reverse else "ssd_scan_fwd",
    )(xc, xc, xc, dt_raw, dt_raw.T, dt_bias.reshape(1, -1).astype(F32),
      dt_bias.reshape(-1, 1).astype(F32), a_log.reshape(1, -1).astype(F32),
      a_log.reshape(-1, 1).astype(F32))


def _gated_norm_body(yf_ref, yb_ref, xs_ref, z_ref, d_ref, g_ref, o_ref):
    z = z_ref[...].astype(F32)
    y = (yf_ref[...].astype(F32) + yb_ref[...].astype(F32)
         + d_ref[...] * xs_ref[...].astype(F32)) * (z * jax.nn.sigmoid(z))
    ms = jnp.mean(y * y, axis=-1, keepdims=True)
    o_ref[...] = (y * lax.rsqrt(ms + RMS_EPS) * g_ref[...]).astype(o_ref.dtype)


def gated_norm(y_f, y_b, xc, proj, z_col_block, d_skip, norm_g, tm=256):
    n, c = y_f.shape
    tm = min(tm, n)
    row = lambda i: (i, 0)
    return pl.pallas_call(
        _gated_norm_body,
        grid=(n // tm,),
        in_specs=[pl.BlockSpec((tm, c), row), pl.BlockSpec((tm, c), row),
                  pl.BlockSpec((tm, c), row),
                  pl.BlockSpec((tm, c), lambda i: (i, z_col_block)),
                  pl.BlockSpec((1, c), lambda i: (0, 0)), pl.BlockSpec((1, c), lambda i: (0, 0))],
        out_specs=pl.BlockSpec((tm, c), row),
        out_shape=jax.ShapeDtypeStruct((n, c), BF16),
        compiler_params=_params("parallel"),
        name="gated_norm",
    )(y_f, y_b, xc, proj, jnp.repeat(d_skip.astype(F32), SSM_HEAD_DIM).reshape(1, c),
      norm_g.reshape(1, c).astype(F32))


def _merge_body(att_ref, m_ref, wa_ref, ws_ref, ga_ref, gs_ref, ba_ref, bs_ref, o_ref):
    a_out = jnp.dot(att_ref[...], wa_ref[...], preferred_element_type=F32)
    m_out = jnp.dot(m_ref[...], ws_ref[...], preferred_element_type=F32)
    g_att = jax.nn.sigmoid(ga_ref[...] + ba_ref[...])
    g_ssm = jax.nn.sigmoid(gs_ref[...] + bs_ref[...])
    o_ref[...] = (g_att * a_out + g_ssm * m_out).astype(o_ref.dtype)


def branch_merge(att, m, w_att, w_ssm, gate_logits, gate_b, tm=512, tn=512):
    n, d = att.shape[0], w_att.shape[1]
    tm = min(tm, n)
    nb = d // tn
    return pl.pallas_call(
        _merge_body,
        grid=(nb, n // tm),
        in_specs=[pl.BlockSpec((tm, att.shape[1]), lambda j, i: (i, 0)),
                  pl.BlockSpec((tm, m.shape[1]), lambda j, i: (i, 0)),
                  pl.BlockSpec((w_att.shape[0], tn), lambda j, i: (0, j)),
                  pl.BlockSpec((w_ssm.shape[0], tn), lambda j, i: (0, j)),
                  pl.BlockSpec((tm, tn), lambda j, i: (i, j)),
                  pl.BlockSpec((tm, tn), lambda j, i: (i, nb + j)),
                  pl.BlockSpec((1, tn), lambda j, i: (0, j)),
                  pl.BlockSpec((1, tn), lambda j, i: (0, nb + j))],
        out_specs=pl.BlockSpec((tm, tn), lambda j, i: (i, j)),
        out_shape=jax.ShapeDtypeStruct((n, d), BF16),
        compiler_params=_params("parallel", "parallel"),
        name="branch_merge",
    )(att, m, w_att, w_ssm, gate_logits, gate_logits, gate_b.reshape(1, -1).astype(F32),
      gate_b.reshape(1, -1).astype(F32))


def _peer_score_body(h_ref, wq_ref, keys_ref, o_ref):
    qry = jnp.dot(h_ref[...], wq_ref[...], preferred_element_type=F32).astype(BF16)
    for hc in range(2 * PEER_HEADS):
        qh = qry[:, hc * LANES:(hc + 1) * LANES]
        o_ref[hc] = lax.dot_general(keys_ref[hc], qh, (((1,), (1,)), ((), ())),
                                    preferred_element_type=F32)


def peer_scores(h, wq, keys, tm=512):
    n, d = h.shape
    tm = min(tm, n)
    nhc = 2 * PEER_HEADS
    return pl.pallas_call(
        _peer_score_body,
        grid=(n // tm,),
        in_specs=[pl.BlockSpec((tm, d), lambda i: (i, 0)),
                  pl.BlockSpec(wq.shape, lambda i: (0, 0)),
                  pl.BlockSpec(keys.shape, lambda i: (0, 0, 0))],
        out_specs=pl.BlockSpec((nhc, PEER_KEYS, tm), lambda i: (0, 0, i)),
        out_shape=jax.ShapeDtypeStruct((nhc, PEER_KEYS, n), F32),
        compiler_params=_params("parallel"),
        name="peer_scores",
    )(h, wq, keys)


def _top16_rows(x):
    r = x.shape[0]
    rid = lax.broadcasted_iota(jnp.int32, x.shape, 0)
    out = []
    for _ in range(PEER_TOPK):
        m = jnp.max(x, axis=0, keepdims=True)
        first = jnp.min(jnp.where(x == m, rid, r), axis=0, keepdims=True)
        x = jnp.where(rid == first, NEG_INF, x)
        out.append(m)
    return out


def _batcher_pairs(n):
    pairs, p = [], 1
    while p < n:
        k = p
        while k >= 1:
            for j in range(k % p, n - k, 2 * k):
                for i in range(min(k, n - j - k)):
                    if (i + j) // (2 * p) == (i + j + k) // (2 * p):
                        pairs.append((i + j, i + j + k))
            k //= 2
        p *= 2
    return pairs


def _top16_sorted(x_ref, idx):
    n = PEER_TOPK
    v = [x_ref[idx, 8 * k:8 * k + 8, :] for k in range(n)]

    def cmpx(i, j):
        v[i], v[j] = jnp.maximum(v[i], v[j]), jnp.minimum(v[i], v[j])

    for i, j in _batcher_pairs(n):
        cmpx(i, j)
    for shift in (4, 2, 1):
        other = [pltpu.roll(vk, shift, axis=0) for vk in v]
        v = [jnp.maximum(v[k], other[n - 1 - k]) for k in range(n)]
        d = n // 2
        while d >= 1:
            for k in range(n):
                if k & d == 0:
                    cmpx(k, k + d)
            d //= 2
    return v


def _shift_first(a, a_max, log2z1):
    return (a - a_max) * LOG2E - log2z1


def _shift_second(b, b_max):
    return (b - b_max) * LOG2E


def _peer_route_body(sc_ref, tau_ref, a2_ref, b2_ref):
    t = sc_ref.shape[2]
    sub = lax.broadcasted_iota(jnp.int32, (8, t), 0)
    for h in range(PEER_HEADS):
        va = _top16_sorted(sc_ref, 2 * h)
        vb = _top16_sorted(sc_ref, 2 * h + 1)

        def stack8(rows):
            acc = jnp.zeros((8, t), F32)
            for i, rw in enumerate(rows):
                acc = jnp.where(sub == i, rw, acc)
            return acc

        def candidates(fa, fb, pad):
            fa_lo, fa_hi = stack8(fa[:8]), stack8(fa[8:])
            fb_hi = stack8(fb[8:])
            groups = [fa_lo + fb[0], fa_hi + fb[0], fa_lo + fb[1]]
            for qq, lim in ((2, 5), (3, 4), (4, 3), (5, 2), (6, 2), (7, 2)):
                groups.append(jnp.where(sub < lim, fa_lo + fb[qq], pad))
            groups.append(fa[0] + fb_hi)
            return jnp.concatenate(groups, axis=0)

        cand = candidates(va, vb, NEG_INF)
        best = _top16_rows(cand)
        z = jnp.zeros((1, t), F32)
        for bv in best:
            z = z + jnp.exp(bv - best[0])
        log2z1 = jnp.log2(z) + 1.0
        a2_ref[h] = _shift_first(sc_ref[2 * h], va[0][0:1, :], log2z1)
        b2_ref[h] = _shift_second(sc_ref[2 * h + 1], vb[0][0:1, :])
        va2 = [_shift_first(v, va[0], log2z1) for v in va]
        vb2 = [_shift_second(v, vb[0]) for v in vb]
        cand2 = candidates(va2, vb2, 0.0)
        tau_ref[h:h + 1, :] = jnp.min(jnp.where(cand >= best[PEER_TOPK - 1], cand2, jnp.inf),
                                      axis=0, keepdims=True)


def peer_route(sc_t, tt=256):
    nhc, kk, n = sc_t.shape
    tt = min(tt, n)
    return pl.pallas_call(
        _peer_route_body,
        grid=(n // tt,),
        in_specs=[pl.BlockSpec((nhc, kk, tt), lambda i: (0, 0, i))],
        out_specs=[pl.BlockSpec((PEER_HEADS, tt), lambda i: (0, i)),
                   pl.BlockSpec((PEER_HEADS, kk, tt), lambda i: (0, 0, i)),
                   pl.BlockSpec((PEER_HEADS, kk, tt), lambda i: (0, 0, i))],
        out_shape=[jax.ShapeDtypeStruct((PEER_HEADS, n), F32),
                   jax.ShapeDtypeStruct((PEER_HEADS, kk, n), F32),
                   jax.ShapeDtypeStruct((PEER_HEADS, kk, n), F32)],
        compiler_params=_params("parallel"),
        name="peer_route",
    )(sc_t)


PEER_PAIR = 2 * PEER_KEYS


def _peer_mix_body(ht_ref, u_ref, vt_ref, a2_ref, b2_ref, tau_ref, x_ref, o_ref,
                   acc_ref, s0_ref, s1_ref, w0_ref, w1_ref, *, tb):
    c = pl.program_id(1)
    npair = u_ref.shape[0]
    tm = ht_ref.shape[1]
    ipp = PEER_PAIR // PEER_KEYS
    s_slots = (s0_ref, s1_ref)
    w_slots = (w0_ref, w1_ref)

    @pl.when(c == 0)
    def _():
        acc_ref[...] = jnp.zeros(acc_ref.shape, F32)

    def score(pair, slot):
        s_slots[slot][...] = jnp.dot(u_ref[pair], ht_ref[...],
                                     preferred_element_type=F32)

    def gate_act(pair, slot):
        for ii in range(ipp):
            i = (c * npair + pair) * ipp + ii
            a_rows = [a2_ref[h, pl.ds(i, 1), :] for h in range(PEER_HEADS)]
            rows = slice(ii * PEER_KEYS, (ii + 1) * PEER_KEYS)
            for t_i in range(tm // tb):
                ln = slice(t_i * tb, (t_i + 1) * tb)
                gate = jnp.zeros((PEER_KEYS, tb), F32)
                for h in range(PEER_HEADS):
                    s2 = a_rows[h][:, ln] + b2_ref[h, :, ln]
                    gate = gate + jnp.where(s2 >= tau_ref[h:h + 1, ln], jnp.exp2(s2), 0.0)
                s_blk = s_slots[slot][rows, ln]
                act = s_blk * (1.0 + lax.erf(s_blk * (2.0 ** -0.5)))
                w_slots[slot][rows, ln] = (act * gate).astype(BF16)

    def mix(pair, slot):
        acc_ref[...] += jnp.dot(vt_ref[pair], w_slots[slot][...],
                                preferred_element_type=F32)

    score(0, 0)

    def trip(q, carry):
        score(2 * q + 1, 1)
        gate_act(2 * q, 0)
        mix(2 * q, 0)
        score(jnp.minimum(2 * q + 2, npair - 1), 0)
        gate_act(2 * q + 1, 1)
        mix(2 * q + 1, 1)
        return carry

    lax.fori_loop(0, npair // 2, trip, 0)

    @pl.when(c == pl.num_programs(1) - 1)
    def _():
        o_ref[...] = x_ref[...] + acc_ref[...].T


def peer_mix(h_t, u3, vt3, a2, b2, tau2, x_res, tm=512, npair=8, tb=128):
    d, n = h_t.shape
    tm = min(tm, n)
    tb = min(tb, tm)
    assert u3.shape[0] % npair == 0 and npair % 2 == 0 and tm % tb == 0
    return pl.pallas_call(
        functools.partial(_peer_mix_body, tb=tb),
        grid=(n // tm, u3.shape[0] // npair),
        in_specs=[pl.BlockSpec((d, tm), lambda t, c: (0, t)),
                  pl.BlockSpec((npair, PEER_PAIR, d), lambda t, c: (c, 0, 0)),
                  pl.BlockSpec((npair, d, PEER_PAIR), lambda t, c: (c, 0, 0)),
                  pl.BlockSpec((PEER_HEADS, PEER_KEYS, tm), lambda t, c: (0, 0, t)),
                  pl.BlockSpec((PEER_HEADS, PEER_KEYS, tm), lambda t, c: (0, 0, t)),
                  pl.BlockSpec((PEER_HEADS, tm), lambda t, c: (0, t)),
                  pl.BlockSpec((tm, d), lambda t, c: (t, 0))],
        out_specs=pl.BlockSpec((tm, d), lambda t, c: (t, 0)),
        out_shape=jax.ShapeDtypeStruct((n, d), F32),
        scratch_shapes=[pltpu.VMEM((d, tm), F32),
                        pltpu.VMEM((PEER_PAIR, tm), F32), pltpu.VMEM((PEER_PAIR, tm), F32),
                        pltpu.VMEM((PEER_PAIR, tm), BF16), pltpu.VMEM((PEER_PAIR, tm), BF16)],
        compiler_params=_params("parallel", "arbitrary"),
        name="peer_mix",
    )(h_t, u3, vt3, a2, b2, tau2, x_res)


def kernel(x, positions, norm1_g, w_in, gate_b, diff_lam, subln_g, w_att_br, conv_w, conv_b,
           dt_bias, a_log, d_skip, ssm_norm_g, w_ssm_br, w_out, norm2_g, peer_wq, peer_keys,
           peer_u, peer_v, final_g):
    batch, seq, d = x.shape
    depth = w_in.shape[0]
    n = batch * seq
    qk_cols = 2 * ATT_HEADS * 2 * ATT_HEAD_DIM
    qkv_cols = 3 * ATT_HEADS * 2 * ATT_HEAD_DIM
    z_cols = SSM_HEADS * SSM_HEAD_DIM
    xbc_cols = z_cols + 2 * SSM_GROUPS * SSM_STATE
    main_cols = qkv_cols + z_cols + xbc_cols
    dt_cols = 2 * SSM_HEADS

    cosf, sin_a, sin_b = rotary_tables(positions)
    q_scale = ATT_HEAD_DIM ** -0.5 * LOG2E
    xf = x.reshape(n, d)

    for l in range(depth):
        lam_init = 0.8 - 0.6 * math.exp(-0.3 * l)
        w_l = w_in[l]
        w_main = jnp.concatenate([w_l[:, :qk_cols], w_l[:, qkv_cols:main_cols],
                                  w_l[:, qk_cols:qkv_cols]], axis=1).astype(BF16)
        w_dt = w_l[:, main_cols:main_cols + dt_cols].astype(BF16)
        w_gate = w_l[:, main_cols + dt_cols:].astype(BF16)

        h1 = rmsnorm(xf, norm1_g[l])
        proj = inproj_rotary(h1, w_main, cosf, sin_a, sin_b, q_scale)
        dt_raw = matmul(h1, w_dt, F32)
        gate_logits = matmul(h1, w_gate, F32)

        att = diff_attention(proj, (qk_cols + z_cols + xbc_cols) // LANES, diff_lam[l],
                             subln_g[l], lam_init, batch, seq)

        xc = conv_silu(proj, qk_cols + z_cols, conv_w[l], conv_b[l], batch, seq)
        y_f = ssd_scan(xc, dt_raw[:, :SSM_HEADS], dt_bias[l, 0], a_log[l, 0], batch, seq, False)
        y_b = ssd_scan(xc, dt_raw[:, SSM_HEADS:], dt_bias[l, 1], a_log[l, 1], batch, seq, True)
        m = gated_norm(y_f, y_b, xc, proj, qk_cols // z_cols, d_skip[l], ssm_norm_g[l])

        merged = branch_merge(att, m, w_att_br[l].astype(BF16), w_ssm_br[l].astype(BF16),
                              gate_logits, gate_b[l])
        xf = matmul(merged, w_out[l].astype(BF16), F32, residual=xf)

        h2 = rmsnorm(xf, norm2_g[l])
        keys = peer_keys[l].reshape(2 * PEER_HEADS, PEER_KEYS, -1).astype(BF16)
        sc_t = peer_scores(h2, peer_wq[l].astype(BF16), keys)
        tau2, a2, b2 = peer_route(sc_t)
        u3 = peer_u[l].astype(BF16).reshape(-1, PEER_PAIR, d)
        vt3 = peer_v[l].astype(BF16).reshape(-1, PEER_PAIR, d).transpose(0, 2, 1)
        xf = peer_mix(h2.T, u3, vt3, a2, b2, tau2, xf)

    return rmsnorm(xf, final_g, out_dtype=x.dtype).reshape(batch, seq, d)
```

```python
import functools
import math

import jax
import jax.numpy as jnp
from jax import lax
from jax.experimental import pallas as pl
from jax.experimental.pallas import tpu as pltpu

F32 = jnp.float32
BF16 = jnp.bfloat16

LANES = 128
VMEM_LIMIT = 48 * 1024 * 1024

ATT_HEADS = 8
ATT_HEAD_DIM = 64
ROPE_DIM = 16
ROPE_THETA = 500000.0
SSM_HEADS = 32
SSM_HEAD_DIM = 64
SSM_GROUPS = 8
SSM_STATE = 128
SSM_CONV = 5
SSM_CHUNK = 128
PEER_HEADS = 8
PEER_KEYS = 128
PEER_TOPK = 16
RMS_EPS = 1e-6
NEG_INF = float("-inf")
LOG2E = 1.4426950408889634


def _params(*sem, flags=None):
    return pltpu.CompilerParams(dimension_semantics=sem, vmem_limit_bytes=VMEM_LIMIT, flags=flags)


def _rmsnorm_body(x_ref, g_ref, o_ref):
    x = x_ref[...].astype(F32)
    ms = jnp.mean(x * x, axis=-1, keepdims=True)
    o_ref[...] = (x * lax.rsqrt(ms + RMS_EPS) * g_ref[...]).astype(o_ref.dtype)


def rmsnorm(x, g, out_dtype=BF16, tm=512):
    n, d = x.shape
    tm = min(tm, n)
    return pl.pallas_call(
        _rmsnorm_body,
        grid=(n // tm,),
        in_specs=[pl.BlockSpec((tm, d), lambda i: (i, 0)),
                  pl.BlockSpec((1, d), lambda i: (0, 0))],
        out_specs=pl.BlockSpec((tm, d), lambda i: (i, 0)),
        out_shape=jax.ShapeDtypeStruct((n, d), out_dtype),
        compiler_params=_params("parallel"),
        name="rmsnorm",
    )(x, g.reshape(1, d).astype(F32))


def _mm_body(a_ref, w_ref, o_ref):
    o_ref[...] = jnp.dot(a_ref[...], w_ref[...], preferred_element_type=F32).astype(o_ref.dtype)


def _mm_res_body(a_ref, w_ref, r_ref, o_ref):
    acc = jnp.dot(a_ref[...], w_ref[...], preferred_element_type=F32)
    o_ref[...] = (r_ref[...].astype(F32) + acc).astype(o_ref.dtype)


def matmul(a, w, out_dtype, residual=None, tm=512, tn=1024):
    m, k = a.shape
    n = w.shape[1]
    tm = min(tm, m)
    tn = min(tn, n)
    assert m % tm == 0 and n % tn == 0
    in_specs = [pl.BlockSpec((tm, k), lambda j, i: (i, 0)),
                pl.BlockSpec((k, tn), lambda j, i: (0, j))]
    args = [a, w]
    body = _mm_body
    if residual is not None:
        in_specs.append(pl.BlockSpec((tm, tn), lambda j, i: (i, j)))
        args.append(residual)
        body = _mm_res_body
    return pl.pallas_call(
        body,
        grid=(n // tn, m // tm),
        in_specs=in_specs,
        out_specs=pl.BlockSpec((tm, tn), lambda j, i: (i, j)),
        out_shape=jax.ShapeDtypeStruct((m, n), out_dtype),
        compiler_params=_params("parallel", "parallel"),
        name="matmul",
    )(*args)


def _inproj_body(a_ref, w_ref, cos_ref, sa_ref, sb_ref, o_ref, *, q_scale):
    j = pl.program_id(0)
    acc = jnp.dot(a_ref[...], w_ref[...], preferred_element_type=F32)

    @pl.when(j >= 2)
    def _():
        o_ref[...] = acc.astype(o_ref.dtype)

    @pl.when(j < 2)
    def _():
        scale = jnp.where(j == 0, q_scale, 1.0).astype(F32)
        cosf = cos_ref[...] * scale
        sa = sa_ref[...] * scale
        sb = sb_ref[...] * scale
        tn = acc.shape[1]
        for g in range(tn // LANES):
            t = acc[:, g * LANES:(g + 1) * LANES]
            half = ROPE_DIM // 2
            r = (t * cosf + pltpu.roll(t, half, axis=1) * sa
                 + pltpu.roll(t, LANES - half, axis=1) * sb)
            o_ref[:, g * LANES:(g + 1) * LANES] = r.astype(o_ref.dtype)


def inproj_rotary(h, w, cosf, sin_a, sin_b, q_scale, tm=512, tn=1024):
    m, k = h.shape
    n = w.shape[1]
    tm = min(tm, m)
    assert m % tm == 0 and n % tn == 0
    tab = pl.BlockSpec((tm, LANES), lambda j, i: (i, 0))
    return pl.pallas_call(
        functools.partial(_inproj_body, q_scale=q_scale),
        grid=(n // tn, m // tm),
        in_specs=[pl.BlockSpec((tm, k), lambda j, i: (i, 0)),
                  pl.BlockSpec((k, tn), lambda j, i: (0, j)),
                  tab, tab, tab],
        out_specs=pl.BlockSpec((tm, tn), lambda j, i: (i, j)),
        out_shape=jax.ShapeDtypeStruct((m, n), BF16),
        compiler_params=_params("parallel", "parallel"),
        name="inproj_rotary",
    )(h, w, cosf, sin_a, sin_b)


def rotary_tables(positions):
    half = ROPE_DIM // 2
    inv_freq = ROPE_THETA ** (-jnp.arange(0, ROPE_DIM, 2, dtype=F32) / ROPE_DIM)
    ang = positions.reshape(-1).astype(F32)[:, None] * inv_freq
    cos, sin = jnp.cos(ang), jnp.sin(ang)
    n = ang.shape[0]
    one = jnp.ones((n, ATT_HEAD_DIM - ROPE_DIM), F32)
    zero8 = jnp.zeros((n, half), F32)
    zero = jnp.zeros((n, ATT_HEAD_DIM - ROPE_DIM), F32)
    cos64 = jnp.concatenate([cos, cos, one], axis=1)
    sa64 = jnp.concatenate([zero8, sin, zero], axis=1)
    sb64 = jnp.concatenate([-sin, zero8, zero], axis=1)
    dup = lambda t: jnp.concatenate([t, t], axis=1)
    return dup(cos64), dup(sa64), dup(sb64)


def _attn_body(lam_ref, g_ref, q_ref, k_ref, v_ref, o_ref, qs_ref, m_ref, acc_ref, s_ref,
               *, tk, lam_init):
    tq = q_ref.shape[0]
    s_len = k_ref.shape[0]
    lane = lax.broadcasted_iota(jnp.int32, (1, LANES), 1)
    first = lane < ATT_HEAD_DIM
    q = q_ref[...]
    zero = jnp.zeros_like(q)
    qs_ref[0:tq, :] = jnp.where(first, q, zero)
    qs_ref[tq:2 * tq, :] = jnp.where(first, zero, q)
    m_ref[...] = jnp.full(m_ref.shape, NEG_INF, F32)
    acc_ref[...] = jnp.zeros(acc_ref.shape, F32)
    ones = jnp.ones((tk, LANES), BF16)

    def scores(kc, slot):
        off = pl.multiple_of(kc * tk, tk)
        s_ref[slot] = lax.dot_general(qs_ref[...], k_ref[pl.ds(off, tk), :],
                                      (((1,), (1,)), ((), ())),
                                      preferred_element_type=F32)

    def update(kc, slot):
        off = pl.multiple_of(kc * tk, tk)
        v_ext = jnp.concatenate([v_ref[pl.ds(off, tk), :], ones], axis=1)
        s = s_ref[slot]
        m_old = m_ref[...]
        m_new = jnp.maximum(m_old, jnp.max(s, axis=-1, keepdims=True))
        alpha = jnp.exp2(m_old - m_new)
        p = jnp.concatenate(
            [jnp.exp2(s[:, j * LANES:(j + 1) * LANES] - m_new) for j in range(tk // LANES)], axis=1)
        pv = jnp.dot(p.astype(BF16), v_ext, preferred_element_type=F32)
        acc_ref[...] = jnp.concatenate([alpha, alpha], axis=1) * acc_ref[...] + pv
        m_ref[...] = m_new

    n_chunks = s_len // tk
    assert n_chunks % 2 == 0
    scores(0, 0)

    def step(i, carry):
        scores(2 * i + 1, 1)
        update(2 * i, 0)
        scores(jnp.minimum(2 * i + 2, n_chunks - 1), 0)
        update(2 * i + 1, 1)
        return carry

    lax.fori_loop(0, n_chunks // 2, step, 0)

    lp = lam_ref[...].astype(F32)
    lam = (jnp.exp(jnp.sum(lp[0:1] * lp[1:2], axis=-1, keepdims=True))
           - jnp.exp(jnp.sum(lp[2:3] * lp[3:4], axis=-1, keepdims=True)) + lam_init)
    a1 = acc_ref[0:tq, :]
    a2 = acc_ref[tq:2 * tq, :]
    out = a1[:, :LANES] / a1[:, LANES:] - lam * (a2[:, :LANES] / a2[:, LANES:])
    ms = jnp.mean(out * out, axis=-1, keepdims=True)
    o_ref[...] = (out * lax.rsqrt(ms + RMS_EPS) * g_ref[...] * (1.0 - lam_init)).astype(o_ref.dtype)


def diff_attention(proj, v_blk, lam_params, subln_g, lam_init, batch, seq, tq=1024, tk=512):
    n = batch * seq
    tq = min(tq, seq)
    tk = min(tk, seq)
    nq = seq // tq
    h = ATT_HEADS
    return pl.pallas_call(
        functools.partial(_attn_body, tk=tk, lam_init=lam_init),
        grid=(batch, h, nq),
        in_specs=[pl.BlockSpec((4, ATT_HEAD_DIM), lambda b, hh, i: (0, 0)),
                  pl.BlockSpec((1, LANES), lambda b, hh, i: (0, 0)),
                  pl.BlockSpec((tq, LANES), lambda b, hh, i: (b * nq + i, hh)),
                  pl.BlockSpec((seq, LANES), lambda b, hh, i: (b, h + hh)),
                  pl.BlockSpec((seq, LANES), lambda b, hh, i: (b, v_blk + hh))],
        out_specs=pl.BlockSpec((tq, LANES), lambda b, hh, i: (b * nq + i, hh)),
        out_shape=jax.ShapeDtypeStruct((n, h * LANES), BF16),
        scratch_shapes=[pltpu.VMEM((2 * tq, LANES), BF16), pltpu.VMEM((2 * tq, LANES), F32),
                        pltpu.VMEM((2 * tq, 2 * LANES), F32),
                        pltpu.VMEM((2, 2 * tq, tk), F32)],
        compiler_params=_params("parallel", "parallel", "parallel"),
        name="diff_attention",
    )(lam_params.astype(F32), subln_g.reshape(1, LANES).astype(F32), proj, proj, proj)


HALO = 8


def _conv_body(prev_ref, cur_ref, next_ref, w_ref, b_ref, o_ref, *, n_seq_blocks):
    i = pl.program_id(1)
    ts = cur_ref.shape[0]
    prev = jnp.where(i > 0, prev_ref[...].astype(F32), 0.0)
    nxt = jnp.where(i < n_seq_blocks - 1, next_ref[...].astype(F32), 0.0)
    ext = jnp.concatenate([prev, cur_ref[...].astype(F32), nxt], axis=0)
    w = w_ref[...]
    acc = jnp.zeros(cur_ref.shape, F32) + b_ref[...]
    pad = SSM_CONV // 2
    for kk in range(SSM_CONV):
        start = HALO - pad + kk
        acc = acc + ext[start:start + ts, :] * w[kk:kk + 1, :]
    o_ref[...] = (acc * jax.nn.sigmoid(acc)).astype(o_ref.dtype)


def conv_silu(proj, col_off, conv_w, conv_b, batch, seq, ts=512, tc=512):
    n = batch * seq
    c = conv_w.shape[1]
    ts = min(ts, seq)
    nsb = seq // ts
    assert col_off % tc == 0 and c % tc == 0 and ts % HALO == 0
    cb = col_off // tc
    hb = ts // HALO
    last_halo = n // HALO - 1

    def prev_map(b, i, j):
        return (jnp.maximum((b * nsb + i) * hb - 1, 0), cb + j)

    def next_map(b, i, j):
        return (jnp.minimum((b * nsb + i + 1) * hb, last_halo), cb + j)

    return pl.pallas_call(
        functools.partial(_conv_body, n_seq_blocks=nsb),
        grid=(batch, nsb, c // tc),
        in_specs=[pl.BlockSpec((HALO, tc), prev_map),
                  pl.BlockSpec((ts, tc), lambda b, i, j: (b * nsb + i, cb + j)),
                  pl.BlockSpec((HALO, tc), next_map),
                  pl.BlockSpec((SSM_CONV, tc), lambda b, i, j: (0, j)),
                  pl.BlockSpec((1, tc), lambda b, i, j: (0, j))],
        out_specs=pl.BlockSpec((ts, tc), lambda b, i, j: (b * nsb + i, j)),
        out_shape=jax.ShapeDtypeStruct((n, c), BF16),
        compiler_params=_params("parallel", "parallel", "parallel"),
        name="conv_silu",
    )(proj, proj, proj, conv_w.astype(F32), conv_b.reshape(1, c).astype(F32))


def _softplus(x):
    return jnp.maximum(x, 0.0) + jnp.log1p(jnp.exp(-jnp.abs(x)))


def _ssd_body(x_ref, b_ref, c_ref, dt_ref, dtt_ref, bias_ref, biast_ref, alog_ref, alogt_ref,
              y_ref, state_ref, *, reverse):
    q = SSM_CHUNK
    hpg = SSM_HEADS // SSM_GROUPS
    gw = hpg * SSM_HEAD_DIM
    hi = lax.Precision.HIGHEST

    @pl.when(pl.program_id(1) == 0)
    def _():
        state_ref[...] = jnp.zeros(state_ref.shape, F32)

    row = lax.broadcasted_iota(jnp.int32, (q, q), 0)
    col = lax.broadcasted_iota(jnp.int32, (q, q), 1)
    keep = (col >= row) if reverse else (col <= row)
    incl = keep.astype(BF16)
    incl_t = ((row >= col) if reverse else (row <= col)).astype(BF16)

    def split(v):
        v_hi = v.astype(BF16)
        return v_hi, (v - v_hi.astype(F32)).astype(BF16)

    dt = _softplus(dt_ref[...] + bias_ref[...])
    dtt = _softplus(dtt_ref[...] + biast_ref[...])
    a = dt * (-jnp.exp(alog_ref[...]))
    at = dtt * (-jnp.exp(alogt_ref[...]))
    a_hi, a_lo = split(a)
    at_hi, at_lo = split(at)
    cum = (jnp.dot(incl, a_hi, preferred_element_type=F32)
           + jnp.dot(incl, a_lo, preferred_element_type=F32))
    cum_t = (jnp.dot(at_hi, incl_t, preferred_element_type=F32)
             + jnp.dot(at_lo, incl_t, preferred_element_type=F32))
    total = jnp.sum(a, axis=0, keepdims=True)

    hid = lax.broadcasted_iota(jnp.int32, (SSM_HEADS, SSM_HEADS * SSM_HEAD_DIM), 0)
    lid = lax.broadcasted_iota(jnp.int32, (SSM_HEADS, SSM_HEADS * SSM_HEAD_DIM), 1)
    expand = (lid // SSM_HEAD_DIM == hid).astype(BF16)

    def ex(v):
        v_hi, v_lo = split(v)
        return (jnp.dot(v_hi, expand, preferred_element_type=F32)
                + jnp.dot(v_lo, expand, preferred_element_type=F32))

    xdec = (x_ref[...].astype(F32) * ex(dt * jnp.exp(total - cum))).astype(BF16)
    dec_out = ex(jnp.exp(cum))
    dec_chunk = ex(jnp.exp(total))

    lane_g = lax.broadcasted_iota(jnp.int32, (1, gw), 1) // SSM_HEAD_DIM

    for g in range(SSM_GROUPS):
        bg = b_ref[:, g * SSM_STATE:(g + 1) * SSM_STATE]
        cg = c_ref[:, g * SSM_STATE:(g + 1) * SSM_STATE]
        cb = lax.dot_general(cg, bg, (((1,), (1,)), ((), ())), preferred_element_type=F32)
        xg = x_ref[:, g * gw:(g + 1) * gw]
        ms, xbd = [], []
        for hh in range(hpg):
            head = g * hpg + hh
            diff = cum[:, head:head + 1] - cum_t[head:head + 1, :]
            decay = jnp.exp(jnp.where(keep, diff, -1e30)) * dtt[head:head + 1, :]
            ms.append((cb * decay).astype(BF16))
            xbd.append(jnp.where(lane_g == hh, xg, jnp.zeros_like(xg)))
        m_cat = jnp.concatenate(ms, axis=1)
        x_bd = jnp.concatenate(xbd, axis=0)
        y_diag = jnp.dot(m_cat, x_bd, preferred_element_type=F32)

        st = state_ref[g]
        y_off = jnp.dot(cg, st.astype(BF16), preferred_element_type=F32) * dec_out[:, g * gw:(g + 1) * gw]
        y_ref[:, g * gw:(g + 1) * gw] = (y_diag + y_off).astype(y_ref.dtype)

        new = lax.dot_general(bg, xdec[:, g * gw:(g + 1) * gw], (((0,), (0,)), ((), ())),
                              preferred_element_type=F32)
        state_ref[g] = st * dec_chunk[:, g * gw:(g + 1) * gw] + new


def ssd_scan(xc, dt_raw, dt_bias, a_log, batch, seq, reverse):
    n = batch * seq
    q = SSM_CHUNK
    nc = seq // q
    hp = SSM_HEADS * SSM_HEAD_DIM
    gn = SSM_GROUPS * SSM_STATE
    assert xc.shape[1] == hp + 2 * gn and hp == 2 * gn

    def cidx(c):
        return nc - 1 - c if reverse else c

    return pl.pallas_call(
        functools.partial(_ssd_body, reverse=reverse),
        grid=(batch, nc),
        in_specs=[pl.BlockSpec((q, hp), lambda b, c: (b * nc + cidx(c), 0)),
                  pl.BlockSpec((q, gn), lambda b, c: (b * nc + cidx(c), 2)),
                  pl.BlockSpec((q, gn), lambda b, c: (b * nc + cidx(c), 3)),
                  pl.BlockSpec((q, SSM_HEADS), lambda b, c: (b * nc + cidx(c), 0)),
                  pl.BlockSpec((SSM_HEADS, q), lambda b, c: (0, b * nc + cidx(c))),
                  pl.BlockSpec((1, SSM_HEADS), lambda b, c: (0, 0)),
                  pl.BlockSpec((SSM_HEADS, 1), lambda b, c: (0, 0)),
                  pl.BlockSpec((1, SSM_HEADS), lambda b, c: (0, 0)),
                  pl.BlockSpec((SSM_HEADS, 1), lambda b, c: (0, 0))],
        out_specs=pl.BlockSpec((q, hp), lambda b, c: (b * nc + cidx(c), 0)),
        out_shape=jax.ShapeDtypeStruct((n, hp), BF16),
        scratch_shapes=[pltpu.VMEM((SSM_GROUPS, SSM_STATE, hp // SSM_GROUPS), F32)],
        compiler_params=_params("parallel", "arbitrary"),
        name="ssd_scan_bwd" if reverse else "ssd_scan_fwd",
    )(xc, xc, xc, dt_raw, dt_raw.T, dt_bias.reshape(1, -1).astype(F32),
      dt_bias.reshape(-1, 1).astype(F32), a_log.reshape(1, -1).astype(F32),
      a_log.reshape(-1, 1).astype(F32))


def _gated_norm_body(yf_ref, yb_ref, xs_ref, z_ref, d_ref, g_ref, o_ref):
    z = z_ref[...].astype(F32)
    y = (yf_ref[...].astype(F32) + yb_ref[...].astype(F32)
         + d_ref[...] * xs_ref[...].astype(F32)) * (z * jax.nn.sigmoid(z))
    ms = jnp.mean(y * y, axis=-1, keepdims=True)
    o_ref[...] = (y * lax.rsqrt(ms + RMS_EPS) * g_ref[...]).astype(o_ref.dtype)


def gated_norm(y_f, y_b, xc, proj, z_col_block, d_skip, norm_g, tm=256):
    n, c = y_f.shape
    tm = min(tm, n)
    row = lambda i: (i, 0)
    return pl.pallas_call(
        _gated_norm_body,
        grid=(n // tm,),
        in_specs=[pl.BlockSpec((tm, c), row), pl.BlockSpec((tm, c), row),
                  pl.BlockSpec((tm, c), row),
                  pl.BlockSpec((tm, c), lambda i: (i, z_col_block)),
                  pl.BlockSpec((1, c), lambda i: (0, 0)), pl.BlockSpec((1, c), lambda i: (0, 0))],
        out_specs=pl.BlockSpec((tm, c), row),
        out_shape=jax.ShapeDtypeStruct((n, c), BF16),
        compiler_params=_params("parallel"),
        name="gated_norm",
    )(y_f, y_b, xc, proj, jnp.repeat(d_skip.astype(F32), SSM_HEAD_DIM).reshape(1, c),
      norm_g.reshape(1, c).astype(F32))


def _merge_body(att_ref, m_ref, wa_ref, ws_ref, ga_ref, gs_ref, ba_ref, bs_ref, o_ref):
    a_out = jnp.dot(att_ref[...], wa_ref[...], preferred_element_type=F32)
    m_out = jnp.dot(m_ref[...], ws_ref[...], preferred_element_type=F32)
    g_att = jax.nn.sigmoid(ga_ref[...] + ba_ref[...])
    g_ssm = jax.nn.sigmoid(gs_ref[...] + bs_ref[...])
    o_ref[...] = (g_att * a_out + g_ssm * m_out).astype(o_ref.dtype)


def branch_merge(att, m, w_att, w_ssm, gate_logits, gate_b, tm=512, tn=512):
    n, d = att.shape[0], w_att.shape[1]
    tm = min(tm, n)
    nb = d // tn
    return pl.pallas_call(
        _merge_body,
        grid=(nb, n // tm),
        in_specs=[pl.BlockSpec((tm, att.shape[1]), lambda j, i: (i, 0)),
                  pl.BlockSpec((tm, m.shape[1]), lambda j, i: (i, 0)),
                  pl.BlockSpec((w_att.shape[0], tn), lambda j, i: (0, j)),
                  pl.BlockSpec((w_ssm.shape[0], tn), lambda j, i: (0, j)),
                  pl.BlockSpec((tm, tn), lambda j, i: (i, j)),
                  pl.BlockSpec((tm, tn), lambda j, i: (i, nb + j)),
                  pl.BlockSpec((1, tn), lambda j, i: (0, j)),
                  pl.BlockSpec((1, tn), lambda j, i: (0, nb + j))],
        out_specs=pl.BlockSpec((tm, tn), lambda j, i: (i, j)),
        out_shape=jax.ShapeDtypeStruct((n, d), BF16),
        compiler_params=_params("parallel", "parallel"),
        name="branch_merge",
    )(att, m, w_att, w_ssm, gate_logits, gate_logits, gate_b.reshape(1, -1).astype(F32),
      gate_b.reshape(1, -1).astype(F32))


def _peer_score_body(h_ref, wq_ref, keys_ref, o_ref):
    qry = jnp.dot(h_ref[...], wq_ref[...], preferred_element_type=F32).astype(BF16)
    for hc in range(2 * PEER_HEADS):
        qh = qry[:, hc * LANES:(hc + 1) * LANES]
        o_ref[hc] = lax.dot_general(keys_ref[hc], qh, (((1,), (1,)), ((), ())),
                                    preferred_element_type=F32)


def peer_scores(h, wq, keys, tm=512):
    n, d = h.shape
    tm = min(tm, n)
    nhc = 2 * PEER_HEADS
    return pl.pallas_call(
        _peer_score_body,
        grid=(n // tm,),
        in_specs=[pl.BlockSpec((tm, d), lambda i: (i, 0)),
                  pl.BlockSpec(wq.shape, lambda i: (0, 0)),
                  pl.BlockSpec(keys.shape, lambda i: (0, 0, 0))],
        out_specs=pl.BlockSpec((nhc, PEER_KEYS, tm), lambda i: (0, 0, i)),
        out_shape=jax.ShapeDtypeStruct((nhc, PEER_KEYS, n), F32),
        compiler_params=_params("parallel"),
        name="peer_scores",
    )(h, wq, keys)


def _batcher_pairs(n):
    pairs, p = [], 1
    while p < n:
        k = p
        while k >= 1:
            for j in range(k % p, n - k, 2 * k):
                for i in range(min(k, n - j - k)):
                    if (i + j) // (2 * p) == (i + j + k) // (2 * p):
                        pairs.append((i + j, i + j + k))
            k //= 2
        p *= 2
    return pairs


def _top16_sorted(v):
    n = PEER_TOPK
    v = list(v)
    assert len(v) == n

    def cmpx(i, j):
        v[i], v[j] = jnp.maximum(v[i], v[j]), jnp.minimum(v[i], v[j])

    for i, j in _batcher_pairs(n):
        cmpx(i, j)
    for shift in (4, 2, 1):
        other = [pltpu.roll(vk, shift, axis=0) for vk in v]
        v = [jnp.maximum(v[k], other[n - 1 - k]) for k in range(n)]
        d = n // 2
        while d >= 1:
            for k in range(n):
                if k & d == 0:
                    cmpx(k, k + d)
            d //= 2
    return v


def _key_groups(x_ref, idx):
    return [x_ref[idx, 8 * k:8 * k + 8, :] for k in range(PEER_KEYS // 8)]


def _shift_first(a, a_max, log2z1):
    return (a - a_max) * LOG2E - log2z1


def _shift_second(b, b_max):
    return (b - b_max) * LOG2E


def _peer_route_body(sc_ref, tau_ref, a2_ref, b2_ref):
    t = sc_ref.shape[2]
    sub = lax.broadcasted_iota(jnp.int32, (8, t), 0)
    for h in range(PEER_HEADS):
        va = _top16_sorted(_key_groups(sc_ref, 2 * h))
        vb = _top16_sorted(_key_groups(sc_ref, 2 * h + 1))

        def stack8(rows):
            acc = jnp.zeros((8, t), F32)
            for i, rw in enumerate(rows):
                acc = jnp.where(sub == i, rw, acc)
            return acc

        def candidates(fa, fb, pad):
            fa_lo, fa_hi = stack8(fa[:8]), stack8(fa[8:])
            fb_hi = stack8(fb[8:])
            groups = [fa_lo + fb[0], fa_hi + fb[0], fa_lo + fb[1]]
            for qq, lim in ((2, 5), (3, 4), (4, 3), (5, 2), (6, 2), (7, 2)):
                groups.append(jnp.where(sub < lim, fa_lo + fb[qq], pad))
            groups.append(fa[0] + fb_hi)
            return groups

        cand = candidates(va, vb, NEG_INF)
        pad = [jnp.full((8, t), NEG_INF, F32)] * (PEER_TOPK - len(cand))
        best = _top16_sorted(cand + pad)
        tau = best[PEER_TOPK - 1]
        best0 = va[0] + vb[0]
        z = jnp.zeros((8, t), F32)
        for g in cand:
            z = z + jnp.where(g >= tau, jnp.exp(g - best0), 0.0)
        z = jnp.sum(z, axis=0, keepdims=True)
        log2z1 = jnp.log2(z) + 1.0
        a2_ref[h] = _shift_first(sc_ref[2 * h], va[0][0:1, :], log2z1)
        b2_ref[h] = _shift_second(sc_ref[2 * h + 1], vb[0][0:1, :])
        va2 = [_shift_first(v, va[0], log2z1) for v in va]
        vb2 = [_shift_second(v, vb[0]) for v in vb]
        cand2 = candidates(va2, vb2, 0.0)
        tau2 = jnp.full((8, t), jnp.inf, F32)
        for g, g2 in zip(cand, cand2):
            tau2 = jnp.minimum(tau2, jnp.where(g >= tau, g2, jnp.inf))
        tau_ref[h:h + 1, :] = jnp.min(tau2, axis=0, keepdims=True)


def peer_route(sc_t, tt=256):
    nhc, kk, n = sc_t.shape
    tt = min(tt, n)
    return pl.pallas_call(
        _peer_route_body,
        grid=(n // tt,),
        in_specs=[pl.BlockSpec((nhc, kk, tt), lambda i: (0, 0, i))],
        out_specs=[pl.BlockSpec((PEER_HEADS, tt), lambda i: (0, i)),
                   pl.BlockSpec((PEER_HEADS, kk, tt), lambda i: (0, 0, i)),
                   pl.BlockSpec((PEER_HEADS, kk, tt), lambda i: (0, 0, i))],
        out_shape=[jax.ShapeDtypeStruct((PEER_HEADS, n), F32),
                   jax.ShapeDtypeStruct((PEER_HEADS, kk, n), F32),
                   jax.ShapeDtypeStruct((PEER_HEADS, kk, n), F32)],
        compiler_params=_params("parallel"),
        name="peer_route",
    )(sc_t)


PEER_PAIR = 2 * PEER_KEYS


def _peer_mix_body(ht_ref, u_ref, vt_ref, a2_ref, b2_ref, tau_ref, x_ref, o_ref,
                   acc_ref, s0_ref, s1_ref, w0_ref, w1_ref, *, tb):
    c = pl.program_id(1)
    npair = u_ref.shape[0]
    tm = ht_ref.shape[1]
    ipp = PEER_PAIR // PEER_KEYS
    s_slots = (s0_ref, s1_ref)
    w_slots = (w0_ref, w1_ref)

    @pl.when(c == 0)
    def _():
        acc_ref[...] = jnp.zeros(acc_ref.shape, F32)

    def score(pair, slot):
        s_slots[slot][...] = jnp.dot(u_ref[pair], ht_ref[...],
                                     preferred_element_type=F32)

    def gate_act(pair, slot):
        for ii in range(ipp):
            i = (c * npair + pair) * ipp + ii
            a_rows = [a2_ref[h, pl.ds(i, 1), :] for h in range(PEER_HEADS)]
            rows = slice(ii * PEER_KEYS, (ii + 1) * PEER_KEYS)
            for t_i in range(tm // tb):
                ln = slice(t_i * tb, (t_i + 1) * tb)
                gate = jnp.zeros((PEER_KEYS, tb), F32)
                for h in range(PEER_HEADS):
                    s2 = a_rows[h][:, ln] + b2_ref[h, :, ln]
                    gate = gate + jnp.where(s2 >= tau_ref[h:h + 1, ln], jnp.exp2(s2), 0.0)
                s_blk = s_slots[slot][rows, ln]
                act = s_blk * (1.0 + lax.erf(s_blk * (2.0 ** -0.5)))
                w_slots[slot][rows, ln] = (act * gate).astype(BF16)

    def mix(pair, slot):
        acc_ref[...] += jnp.dot(vt_ref[pair], w_slots[slot][...],
                                preferred_element_type=F32)

    score(0, 0)

    def trip(q, carry):
        score(2 * q + 1, 1)
        gate_act(2 * q, 0)
        mix(2 * q, 0)
        score(jnp.minimum(2 * q + 2, npair - 1), 0)
        gate_act(2 * q + 1, 1)
        mix(2 * q + 1, 1)
        return carry

    lax.fori_loop(0, npair // 2, trip, 0)

    @pl.when(c == pl.num_programs(1) - 1)
    def _():
        o_ref[...] = x_ref[...] + acc_ref[...].T


def peer_mix(h_t, u3, vt3, a2, b2, tau2, x_res, tm=512, npair=8, tb=128):
    d, n = h_t.shape
    tm = min(tm, n)
    tb = min(tb, tm)
    assert u3.shape[0] % npair == 0 and npair % 2 == 0 and tm % tb == 0
    return pl.pallas_call(
        functools.partial(_peer_mix_body, tb=tb),
        grid=(n // tm, u3.shape[0] // npair),
        in_specs=[pl.BlockSpec((d, tm), lambda t, c: (0, t)),
                  pl.BlockSpec((npair, PEER_PAIR, d), lambda t, c: (c, 0, 0)),
                  pl.BlockSpec((npair, d, PEER_PAIR), lambda t, c: (c, 0, 0)),
                  pl.BlockSpec((PEER_HEADS, PEER_KEYS, tm), lambda t, c: (0, 0, t)),
                  pl.BlockSpec((PEER_HEADS, PEER_KEYS, tm), lambda t, c: (0, 0, t)),
                  pl.BlockSpec((PEER_HEADS, tm), lambda t, c: (0, t)),
                  pl.BlockSpec((tm, d), lambda t, c: (t, 0))],
        out_specs=pl.BlockSpec((tm, d), lambda t, c: (t, 0)),
        out_shape=jax.ShapeDtypeStruct((n, d), F32),
        scratch_shapes=[pltpu.VMEM((d, tm), F32),
                        pltpu.VMEM((PEER_PAIR, tm), F32), pltpu.VMEM((PEER_PAIR, tm), F32),
                        pltpu.VMEM((PEER_PAIR, tm), BF16), pltpu.VMEM((PEER_PAIR, tm), BF16)],
        compiler_params=_params("parallel", "arbitrary"),
        name="peer_mix",
    )(h_t, u3, vt3, a2, b2, tau2, x_res)


def kernel(x, positions, norm1_g, w_in, gate_b, diff_lam, subln_g, w_att_br, conv_w, conv_b,
           dt_bias, a_log, d_skip, ssm_norm_g, w_ssm_br, w_out, norm2_g, peer_wq, peer_keys,
           peer_u, peer_v, final_g):
    batch, seq, d = x.shape
    depth = w_in.shape[0]
    n = batch * seq
    qk_cols = 2 * ATT_HEADS * 2 * ATT_HEAD_DIM
    qkv_cols = 3 * ATT_HEADS * 2 * ATT_HEAD_DIM
    z_cols = SSM_HEADS * SSM_HEAD_DIM
    xbc_cols = z_cols + 2 * SSM_GROUPS * SSM_STATE
    main_cols = qkv_cols + z_cols + xbc_cols
    dt_cols = 2 * SSM_HEADS

    cosf, sin_a, sin_b = rotary_tables(positions)
    q_scale = ATT_HEAD_DIM ** -0.5 * LOG2E
    xf = x.reshape(n, d)

    for l in range(depth):
        lam_init = 0.8 - 0.6 * math.exp(-0.3 * l)
        w_l = w_in[l]
        w_main = jnp.concatenate([w_l[:, :qk_cols], w_l[:, qkv_cols:main_cols],
                                  w_l[:, qk_cols:qkv_cols]], axis=1).astype(BF16)
        w_dt = w_l[:, main_cols:main_cols + dt_cols].astype(BF16)
        w_gate = w_l[:, main_cols + dt_cols:].astype(BF16)

        h1 = rmsnorm(xf, norm1_g[l])
        proj = inproj_rotary(h1, w_main, cosf, sin_a, sin_b, q_scale)
        dt_raw = matmul(h1, w_dt, F32)
        gate_logits = matmul(h1, w_gate, F32)

        att = diff_attention(proj, (qk_cols + z_cols + xbc_cols) // LANES, diff_lam[l],
                             subln_g[l], lam_init, batch, seq)

        xc = conv_silu(proj, qk_cols + z_cols, conv_w[l], conv_b[l], batch, seq)
        y_f = ssd_scan(xc, dt_raw[:, :SSM_HEADS], dt_bias[l, 0], a_log[l, 0], batch, seq, False)
        y_b = ssd_scan(xc, dt_raw[:, SSM_HEADS:], dt_bias[l, 1], a_log[l, 1], batch, seq, True)
        m = gated_norm(y_f, y_b, xc, proj, qk_cols // z_cols, d_skip[l], ssm_norm_g[l])

        merged = branch_merge(att, m, w_att_br[l].astype(BF16), w_ssm_br[l].astype(BF16),
                              gate_logits, gate_b[l])
        xf = matmul(merged, w_out[l].astype(BF16), F32, residual=xf)

        h2 = rmsnorm(xf, norm2_g[l])
        keys = peer_keys[l].reshape(2 * PEER_HEADS, PEER_KEYS, -1).astype(BF16)
        sc_t = peer_scores(h2, peer_wq[l].astype(BF16), keys)
        tau2, a2, b2 = peer_route(sc_t)
        u3 = peer_u[l].astype(BF16).reshape(-1, PEER_PAIR, d)
        vt3 = peer_v[l].astype(BF16).reshape(-1, PEER_PAIR, d).transpose(0, 2, 1)
        xf = peer_mix(h2.T, u3, vt3, a2, b2, tau2, xf)

    return rmsnorm(xf, final_g, out_dtype=x.dtype).reshape(batch, seq, d)
```

```python
import functools
import math

import jax
import jax.numpy as jnp
from jax import lax
from jax.experimental import pallas as pl
from jax.experimental.pallas import tpu as pltpu

F32 = jnp.float32
BF16 = jnp.bfloat16

LANES = 128
VMEM_LIMIT = 48 * 1024 * 1024

ATT_HEADS = 8
ATT_HEAD_DIM = 64
ROPE_DIM = 16
ROPE_THETA = 500000.0
SSM_HEADS = 32
SSM_HEAD_DIM = 64
SSM_GROUPS = 8
SSM_STATE = 128
SSM_CONV = 5
SSM_CHUNK = 128
PEER_HEADS = 8
PEER_KEYS = 128
PEER_TOPK = 16
RMS_EPS = 1e-6
NEG_INF = float("-inf")
LOG2E = 1.4426950408889634


def _params(*sem, flags=None):
    return pltpu.CompilerParams(dimension_semantics=sem, vmem_limit_bytes=VMEM_LIMIT, flags=flags)


def _rmsnorm_body(x_ref, g_ref, o_ref):
    x = x_ref[...].astype(F32)
    ms = jnp.mean(x * x, axis=-1, keepdims=True)
    o_ref[...] = (x * lax.rsqrt(ms + RMS_EPS) * g_ref[...]).astype(o_ref.dtype)


def rmsnorm(x, g, out_dtype=BF16, tm=512):
    n, d = x.shape
    tm = min(tm, n)
    return pl.pallas_call(
        _rmsnorm_body,
        grid=(n // tm,),
        in_specs=[pl.BlockSpec((tm, d), lambda i: (i, 0)),
                  pl.BlockSpec((1, d), lambda i: (0, 0))],
        out_specs=pl.BlockSpec((tm, d), lambda i: (i, 0)),
        out_shape=jax.ShapeDtypeStruct((n, d), out_dtype),
        compiler_params=_params("parallel"),
        name="rmsnorm",
    )(x, g.reshape(1, d).astype(F32))


def _mm_body(a_ref, w_ref, o_ref):
    o_ref[...] = jnp.dot(a_ref[...], w_ref[...], preferred_element_type=F32).astype(o_ref.dtype)


def _mm_res_body(a_ref, w_ref, r_ref, o_ref):
    acc = jnp.dot(a_ref[...], w_ref[...], preferred_element_type=F32)
    o_ref[...] = (r_ref[...].astype(F32) + acc).astype(o_ref.dtype)


def matmul(a, w, out_dtype, residual=None, tm=512, tn=1024):
    m, k = a.shape
    n = w.shape[1]
    tm = min(tm, m)
    tn = min(tn, n)
    assert m % tm == 0 and n % tn == 0
    in_specs = [pl.BlockSpec((tm, k), lambda j, i: (i, 0)),
                pl.BlockSpec((k, tn), lambda j, i: (0, j))]
    args = [a, w]
    body = _mm_body
    if residual is not None:
        in_specs.append(pl.BlockSpec((tm, tn), lambda j, i: (i, j)))
        args.append(residual)
        body = _mm_res_body
    return pl.pallas_call(
        body,
        grid=(n // tn, m // tm),
        in_specs=in_specs,
        out_specs=pl.BlockSpec((tm, tn), lambda j, i: (i, j)),
        out_shape=jax.ShapeDtypeStruct((m, n), out_dtype),
        compiler_params=_params("parallel", "parallel"),
        name="matmul",
    )(*args)


def _inproj_body(a_ref, w_ref, cos_ref, sa_ref, sb_ref, o_ref, *, q_scale):
    j = pl.program_id(0)
    acc = jnp.dot(a_ref[...], w_ref[...], preferred_element_type=F32)

    @pl.when(j >= 2)
    def _():
        o_ref[...] = acc.astype(o_ref.dtype)

    @pl.when(j < 2)
    def _():
        scale = jnp.where(j == 0, q_scale, 1.0).astype(F32)
        cosf = cos_ref[...] * scale
        sa = sa_ref[...] * scale
        sb = sb_ref[...] * scale
        tn = acc.shape[1]
        for g in range(tn // LANES):
            t = acc[:, g * LANES:(g + 1) * LANES]
            half = ROPE_DIM // 2
            r = (t * cosf + pltpu.roll(t, half, axis=1) * sa
                 + pltpu.roll(t, LANES - half, axis=1) * sb)
            o_ref[:, g * LANES:(g + 1) * LANES] = r.astype(o_ref.dtype)


def inproj_rotary(h, w, cosf, sin_a, sin_b, q_scale, tm=512, tn=1024):
    m, k = h.shape
    n = w.shape[1]
    tm = min(tm, m)
    assert m % tm == 0 and n % tn == 0
    tab = pl.BlockSpec((tm, LANES), lambda j, i: (i, 0))
    return pl.pallas_call(
        functools.partial(_inproj_body, q_scale=q_scale),
        grid=(n // tn, m // tm),
        in_specs=[pl.BlockSpec((tm, k), lambda j, i: (i, 0)),
                  pl.BlockSpec((k, tn), lambda j, i: (0, j)),
                  tab, tab, tab],
        out_specs=pl.BlockSpec((tm, tn), lambda j, i: (i, j)),
        out_shape=jax.ShapeDtypeStruct((m, n), BF16),
        compiler_params=_params("parallel", "parallel"),
        name="inproj_rotary",
    )(h, w, cosf, sin_a, sin_b)


def rotary_tables(positions):
    half = ROPE_DIM // 2
    inv_freq = ROPE_THETA ** (-jnp.arange(0, ROPE_DIM, 2, dtype=F32) / ROPE_DIM)
    ang = positions.reshape(-1).astype(F32)[:, None] * inv_freq
    cos, sin = jnp.cos(ang), jnp.sin(ang)
    n = ang.shape[0]
    one = jnp.ones((n, ATT_HEAD_DIM - ROPE_DIM), F32)
    zero8 = jnp.zeros((n, half), F32)
    zero = jnp.zeros((n, ATT_HEAD_DIM - ROPE_DIM), F32)
    cos64 = jnp.concatenate([cos, cos, one], axis=1)
    sa64 = jnp.concatenate([zero8, sin, zero], axis=1)
    sb64 = jnp.concatenate([-sin, zero8, zero], axis=1)
    dup = lambda t: jnp.concatenate([t, t], axis=1)
    return dup(cos64), dup(sa64), dup(sb64)


def _attn_body(lam_ref, g_ref, q_ref, k_ref, v_ref, o_ref, qs_ref, m_ref, acc_ref, s_ref,
               *, tk, lam_init):
    tq = q_ref.shape[0]
    s_len = k_ref.shape[0]
    lane = lax.broadcasted_iota(jnp.int32, (1, LANES), 1)
    first = lane < ATT_HEAD_DIM
    q = q_ref[...]
    zero = jnp.zeros_like(q)
    qs_ref[0:tq, :] = jnp.where(first, q, zero)
    qs_ref[tq:2 * tq, :] = jnp.where(first, zero, q)
    m_ref[...] = jnp.full(m_ref.shape, NEG_INF, F32)
    acc_ref[...] = jnp.zeros(acc_ref.shape, F32)
    ones = jnp.ones((tk, LANES), BF16)

    def scores(kc, slot):
        off = pl.multiple_of(kc * tk, tk)
        s_ref[slot] = lax.dot_general(qs_ref[...], k_ref[pl.ds(off, tk), :],
                                      (((1,), (1,)), ((), ())),
                                      preferred_element_type=F32)

    def update(kc, slot):
        off = pl.multiple_of(kc * tk, tk)
        v_ext = jnp.concatenate([v_ref[pl.ds(off, tk), :], ones], axis=1)
        s = s_ref[slot]
        m_old = m_ref[...]
        m_new = jnp.maximum(m_old, jnp.max(s, axis=-1, keepdims=True))
        alpha = jnp.exp2(m_old - m_new)
        p = jnp.concatenate(
            [jnp.exp2(s[:, j * LANES:(j + 1) * LANES] - m_new) for j in range(tk // LANES)], axis=1)
        pv = jnp.dot(p.astype(BF16), v_ext, preferred_element_type=F32)
        acc_ref[...] = jnp.concatenate([alpha, alpha], axis=1) * acc_ref[...] + pv
        m_ref[...] = m_new

    n_chunks = s_len // tk
    assert n_chunks % 2 == 0
    scores(0, 0)

    def step(i, carry):
        scores(2 * i + 1, 1)
        update(2 * i, 0)
        scores(jnp.minimum(2 * i + 2, n_chunks - 1), 0)
        update(2 * i + 1, 1)
        return carry

    lax.fori_loop(0, n_chunks // 2, step, 0)

    lp = lam_ref[...].astype(F32)
    lam = (jnp.exp(jnp.sum(lp[0:1] * lp[1:2], axis=-1, keepdims=True))
           - jnp.exp(jnp.sum(lp[2:3] * lp[3:4], axis=-1, keepdims=True)) + lam_init)
    a1 = acc_ref[0:tq, :]
    a2 = acc_ref[tq:2 * tq, :]
    out = a1[:, :LANES] / a1[:, LANES:] - lam * (a2[:, :LANES] / a2[:, LANES:])
    ms = jnp.mean(out * out, axis=-1, keepdims=True)
    o_ref[...] = (out * lax.rsqrt(ms + RMS_EPS) * g_ref[...] * (1.0 - lam_init)).astype(o_ref.dtype)


def diff_attention(proj, v_blk, lam_params, subln_g, lam_init, batch, seq, tq=2048, tk=512):
    n = batch * seq
    tq = min(tq, seq)
    tk = min(tk, seq)
    nq = seq // tq
    h = ATT_HEADS
    return pl.pallas_call(
        functools.partial(_attn_body, tk=tk, lam_init=lam_init),
        grid=(batch, h, nq),
        in_specs=[pl.BlockSpec((4, ATT_HEAD_DIM), lambda b, hh, i: (0, 0)),
                  pl.BlockSpec((1, LANES), lambda b, hh, i: (0, 0)),
                  pl.BlockSpec((tq, LANES), lambda b, hh, i: (b * nq + i, hh)),
                  pl.BlockSpec((seq, LANES), lambda b, hh, i: (b, h + hh)),
                  pl.BlockSpec((seq, LANES), lambda b, hh, i: (b, v_blk + hh))],
        out_specs=pl.BlockSpec((tq, LANES), lambda b, hh, i: (b * nq + i, hh)),
        out_shape=jax.ShapeDtypeStruct((n, h * LANES), BF16),
        scratch_shapes=[pltpu.VMEM((2 * tq, LANES), BF16), pltpu.VMEM((2 * tq, LANES), F32),
                        pltpu.VMEM((2 * tq, 2 * LANES), F32),
                        pltpu.VMEM((2, 2 * tq, tk), F32)],
        compiler_params=_params("parallel", "parallel", "parallel"),
        name="diff_attention",
    )(lam_params.astype(F32), subln_g.reshape(1, LANES).astype(F32), proj, proj, proj)


HALO = 8


def _conv_body(prev_ref, cur_ref, next_ref, w_ref, b_ref, o_ref, *, n_seq_blocks):
    i = pl.program_id(1)
    ts = cur_ref.shape[0]
    prev = jnp.where(i > 0, prev_ref[...].astype(F32), 0.0)
    nxt = jnp.where(i < n_seq_blocks - 1, next_ref[...].astype(F32), 0.0)
    ext = jnp.concatenate([prev, cur_ref[...].astype(F32), nxt], axis=0)
    w = w_ref[...]
    acc = jnp.zeros(cur_ref.shape, F32) + b_ref[...]
    pad = SSM_CONV // 2
    for kk in range(SSM_CONV):
        start = HALO - pad + kk
        acc = acc + ext[start:start + ts, :] * w[kk:kk + 1, :]
    o_ref[...] = (acc * jax.nn.sigmoid(acc)).astype(o_ref.dtype)


def conv_silu(proj, col_off, conv_w, conv_b, batch, seq, ts=512, tc=512):
    n = batch * seq
    c = conv_w.shape[1]
    ts = min(ts, seq)
    nsb = seq // ts
    assert col_off % tc == 0 and c % tc == 0 and ts % HALO == 0
    cb = col_off // tc
    hb = ts // HALO
    last_halo = n // HALO - 1

    def prev_map(b, i, j):
        return (jnp.maximum((b * nsb + i) * hb - 1, 0), cb + j)

    def next_map(b, i, j):
        return (jnp.minimum((b * nsb + i + 1) * hb, last_halo), cb + j)

    return pl.pallas_call(
        functools.partial(_conv_body, n_seq_blocks=nsb),
        grid=(batch, nsb, c // tc),
        in_specs=[pl.BlockSpec((HALO, tc), prev_map),
                  pl.BlockSpec((ts, tc), lambda b, i, j: (b * nsb + i, cb + j)),
                  pl.BlockSpec((HALO, tc), next_map),
                  pl.BlockSpec((SSM_CONV, tc), lambda b, i, j: (0, j)),
                  pl.BlockSpec((1, tc), lambda b, i, j: (0, j))],
        out_specs=pl.BlockSpec((ts, tc), lambda b, i, j: (b * nsb + i, j)),
        out_shape=jax.ShapeDtypeStruct((n, c), BF16),
        compiler_params=_params("parallel", "parallel", "parallel"),
        name="conv_silu",
    )(proj, proj, proj, conv_w.astype(F32), conv_b.reshape(1, c).astype(F32))


def _softplus(x):
    return jnp.maximum(x, 0.0) + jnp.log1p(jnp.exp(-jnp.abs(x)))


def _ssd_body(x_ref, b_ref, c_ref, dt_ref, dtt_ref, bias_ref, biast_ref, alog_ref, alogt_ref,
              y_ref, state_ref, *, reverse):
    q = SSM_CHUNK
    hpg = SSM_HEADS // SSM_GROUPS
    gw = hpg * SSM_HEAD_DIM
    hi = lax.Precision.HIGHEST

    @pl.when(pl.program_id(1) == 0)
    def _():
        state_ref[...] = jnp.zeros(state_ref.shape, F32)

    row = lax.broadcasted_iota(jnp.int32, (q, q), 0)
    col = lax.broadcasted_iota(jnp.int32, (q, q), 1)
    keep = (col >= row) if reverse else (col <= row)
    incl = keep.astype(BF16)
    incl_t = ((row >= col) if reverse else (row <= col)).astype(BF16)

    def split(v):
        v_hi = v.astype(BF16)
        return v_hi, (v - v_hi.astype(F32)).astype(BF16)

    dt = _softplus(dt_ref[...] + bias_ref[...])
    dtt = _softplus(dtt_ref[...] + biast_ref[...])
    a = dt * (-jnp.exp(alog_ref[...]))
    at = dtt * (-jnp.exp(alogt_ref[...]))
    a_hi, a_lo = split(a)
    at_hi, at_lo = split(at)
    cum = (jnp.dot(incl, a_hi, preferred_element_type=F32)
           + jnp.dot(incl, a_lo, preferred_element_type=F32))
    cum_t = (jnp.dot(at_hi, incl_t, preferred_element_type=F32)
             + jnp.dot(at_lo, incl_t, preferred_element_type=F32))
    total = jnp.sum(a, axis=0, keepdims=True)

    hid = lax.broadcasted_iota(jnp.int32, (SSM_HEADS, SSM_HEADS * SSM_HEAD_DIM), 0)
    lid = lax.broadcasted_iota(jnp.int32, (SSM_HEADS, SSM_HEADS * SSM_HEAD_DIM), 1)
    expand = (lid // SSM_HEAD_DIM == hid).astype(BF16)

    def ex(v):
        v_hi, v_lo = split(v)
        return (jnp.dot(v_hi, expand, preferred_element_type=F32)
                + jnp.dot(v_lo, expand, preferred_element_type=F32))

    xdec = (x_ref[...].astype(F32) * ex(dt * jnp.exp(total - cum))).astype(BF16)
    dec_out = ex(jnp.exp(cum))
    dec_chunk = ex(jnp.exp(total))

    lane_g = lax.broadcasted_iota(jnp.int32, (1, gw), 1) // SSM_HEAD_DIM

    for g in range(SSM_GROUPS):
        bg = b_ref[:, g * SSM_STATE:(g + 1) * SSM_STATE]
        cg = c_ref[:, g * SSM_STATE:(g + 1) * SSM_STATE]
        cb = lax.dot_general(cg, bg, (((1,), (1,)), ((), ())), preferred_element_type=F32)
        xg = x_ref[:, g * gw:(g + 1) * gw]
        ms, xbd = [], []
        for hh in range(hpg):
            head = g * hpg + hh
            diff = cum[:, head:head + 1] - cum_t[head:head + 1, :]
            decay = jnp.exp(jnp.where(keep, diff, -1e30)) * dtt[head:head + 1, :]
            ms.append((cb * decay).astype(BF16))
            xbd.append(jnp.where(lane_g == hh, xg, jnp.zeros_like(xg)))
        m_cat = jnp.concatenate(ms, axis=1)
        x_bd = jnp.concatenate(xbd, axis=0)
        y_diag = jnp.dot(m_cat, x_bd, preferred_element_type=F32)

        st = state_ref[g]
        y_off = jnp.dot(cg, st.astype(BF16), preferred_element_type=F32) * dec_out[:, g * gw:(g + 1) * gw]
        y_ref[:, g * gw:(g + 1) * gw] = (y_diag + y_off).astype(y_ref.dtype)

        new = lax.dot_general(bg, xdec[:, g * gw:(g + 1) * gw], (((0,), (0,)), ((), ())),
                              preferred_element_type=F32)
        state_ref[g] = st * dec_chunk[:, g * gw:(g + 1) * gw] + new


def ssd_scan(xc, dt_raw, dt_bias, a_log, batch, seq, reverse):
    n = batch * seq
    q = SSM_CHUNK
    nc = seq // q
    hp = SSM_HEADS * SSM_HEAD_DIM
    gn = SSM_GROUPS * SSM_STATE
    assert xc.shape[1] == hp + 2 * gn and hp == 2 * gn

    def cidx(c):
        return nc - 1 - c if reverse else c

    return pl.pallas_call(
        functools.partial(_ssd_body, reverse=reverse),
        grid=(batch, nc),
        in_specs=[pl.BlockSpec((q, hp), lambda b, c: (b * nc + cidx(c), 0)),
                  pl.BlockSpec((q, gn), lambda b, c: (b * nc + cidx(c), 2)),
                  pl.BlockSpec((q, gn), lambda b, c: (b * nc + cidx(c), 3)),
                  pl.BlockSpec((q, SSM_HEADS), lambda b, c: (b * nc + cidx(c), 0)),
                  pl.BlockSpec((SSM_HEADS, q), lambda b, c: (0, b * nc + cidx(c))),
                  pl.BlockSpec((1, SSM_HEADS), lambda b, c: (0, 0)),
                  pl.BlockSpec((SSM_HEADS, 1), lambda b, c: (0, 0)),
                  pl.BlockSpec((1, SSM_HEADS), lambda b, c: (0, 0)),
                  pl.BlockSpec((SSM_HEADS, 1), lambda b, c: (0, 0))],
        out_specs=pl.BlockSpec((q, hp), lambda b, c: (b * nc + cidx(c), 0)),
        out_shape=jax.ShapeDtypeStruct((n, hp), BF16),
        scratch_shapes=[pltpu.VMEM((SSM_GROUPS, SSM_STATE, hp // SSM_GROUPS), F32)],
        compiler_params=_params("parallel", "arbitrary"),
        name="ssd_scan_bwd" if reverse else "ssd_scan_fwd",
    )(xc, xc, xc, dt_raw, dt_raw.T, dt_bias.reshape(1, -1).astype(F32),
      dt_bias.reshape(-1, 1).astype(F32), a_log.reshape(1, -1).astype(F32),
      a_log.reshape(-1, 1).astype(F32))


def _gated_norm_body(yf_ref, yb_ref, xs_ref, z_ref, d_ref, g_ref, o_ref):
    z = z_ref[...].astype(F32)
    y = (yf_ref[...].astype(F32) + yb_ref[...].astype(F32)
         + d_ref[...] * xs_ref[...].astype(F32)) * (z * jax.nn.sigmoid(z))
    ms = jnp.mean(y * y, axis=-1, keepdims=True)
    o_ref[...] = (y * lax.rsqrt(ms + RMS_EPS) * g_ref[...]).astype(o_ref.dtype)


def gated_norm(y_f, y_b, xc, proj, z_col_block, d_skip, norm_g, tm=256):
    n, c = y_f.shape
    tm = min(tm, n)
    row = lambda i: (i, 0)
    return pl.pallas_call(
        _gated_norm_body,
        grid=(n // tm,),
        in_specs=[pl.BlockSpec((tm, c), row), pl.BlockSpec((tm, c), row),
                  pl.BlockSpec((tm, c), row),
                  pl.BlockSpec((tm, c), lambda i: (i, z_col_block)),
                  pl.BlockSpec((1, c), lambda i: (0, 0)), pl.BlockSpec((1, c), lambda i: (0, 0))],
        out_specs=pl.BlockSpec((tm, c), row),
        out_shape=jax.ShapeDtypeStruct((n, c), BF16),
        compiler_params=_params("parallel"),
        name="gated_norm",
    )(y_f, y_b, xc, proj, jnp.repeat(d_skip.astype(F32), SSM_HEAD_DIM).reshape(1, c),
      norm_g.reshape(1, c).astype(F32))


def _merge_body(att_ref, m_ref, wa_ref, ws_ref, ga_ref, gs_ref, ba_ref, bs_ref, o_ref):
    a_out = jnp.dot(att_ref[...], wa_ref[...], preferred_element_type=F32)
    m_out = jnp.dot(m_ref[...], ws_ref[...], preferred_element_type=F32)
    g_att = jax.nn.sigmoid(ga_ref[...] + ba_ref[...])
    g_ssm = jax.nn.sigmoid(gs_ref[...] + bs_ref[...])
    o_ref[...] = (g_att * a_out + g_ssm * m_out).astype(o_ref.dtype)


def branch_merge(att, m, w_att, w_ssm, gate_logits, gate_b, tm=512, tn=512):
    n, d = att.shape[0], w_att.shape[1]
    tm = min(tm, n)
    nb = d // tn
    return pl.pallas_call(
        _merge_body,
        grid=(nb, n // tm),
        in_specs=[pl.BlockSpec((tm, att.shape[1]), lambda j, i: (i, 0)),
                  pl.BlockSpec((tm, m.shape[1]), lambda j, i: (i, 0)),
                  pl.BlockSpec((w_att.shape[0], tn), lambda j, i: (0, j)),
                  pl.BlockSpec((w_ssm.shape[0], tn), lambda j, i: (0, j)),
                  pl.BlockSpec((tm, tn), lambda j, i: (i, j)),
                  pl.BlockSpec((tm, tn), lambda j, i: (i, nb + j)),
                  pl.BlockSpec((1, tn), lambda j, i: (0, j)),
                  pl.BlockSpec((1, tn), lambda j, i: (0, nb + j))],
        out_specs=pl.BlockSpec((tm, tn), lambda j, i: (i, j)),
        out_shape=jax.ShapeDtypeStruct((n, d), BF16),
        compiler_params=_params("parallel", "parallel"),
        name="branch_merge",
    )(att, m, w_att, w_ssm, gate_logits, gate_logits, gate_b.reshape(1, -1).astype(F32),
      gate_b.reshape(1, -1).astype(F32))


def _peer_score_body(h_ref, wq_ref, keys_ref, o_ref):
    qry = jnp.dot(h_ref[...], wq_ref[...], preferred_element_type=F32).astype(BF16)
    for hc in range(2 * PEER_HEADS):
        qh = qry[:, hc * LANES:(hc + 1) * LANES]
        o_ref[hc] = lax.dot_general(keys_ref[hc], qh, (((1,), (1,)), ((), ())),
                                    preferred_element_type=F32)


def peer_scores(h, wq, keys, tm=512):
    n, d = h.shape
    tm = min(tm, n)
    nhc = 2 * PEER_HEADS
    return pl.pallas_call(
        _peer_score_body,
        grid=(n // tm,),
        in_specs=[pl.BlockSpec((tm, d), lambda i: (i, 0)),
                  pl.BlockSpec(wq.shape, lambda i: (0, 0)),
                  pl.BlockSpec(keys.shape, lambda i: (0, 0, 0))],
        out_specs=pl.BlockSpec((nhc, PEER_KEYS, tm), lambda i: (0, 0, i)),
        out_shape=jax.ShapeDtypeStruct((nhc, PEER_KEYS, n), F32),
        compiler_params=_params("parallel"),
        name="peer_scores",
    )(h, wq, keys)


def _batcher_pairs(n):
    pairs, p = [], 1
    while p < n:
        k = p
        while k >= 1:
            for j in range(k % p, n - k, 2 * k):
                for i in range(min(k, n - j - k)):
                    if (i + j) // (2 * p) == (i + j + k) // (2 * p):
                        pairs.append((i + j, i + j + k))
            k //= 2
        p *= 2
    return pairs


def _top16_sorted(v):
    n = PEER_TOPK
    v = list(v)
    assert len(v) == n

    def cmpx(i, j):
        v[i], v[j] = jnp.maximum(v[i], v[j]), jnp.minimum(v[i], v[j])

    for i, j in _batcher_pairs(n):
        cmpx(i, j)
    for shift in (4, 2, 1):
        other = [pltpu.roll(vk, shift, axis=0) for vk in v]
        v = [jnp.maximum(v[k], other[n - 1 - k]) for k in range(n)]
        d = n // 2
        while d >= 1:
            for k in range(n):
                if k & d == 0:
                    cmpx(k, k + d)
            d //= 2
    return v


def _key_groups(x_ref, idx):
    return [x_ref[idx, 8 * k:8 * k + 8, :] for k in range(PEER_KEYS // 8)]


def _shift_first(a, a_max, log2z1):
    return (a - a_max) * LOG2E - log2z1


def _shift_second(b, b_max):
    return (b - b_max) * LOG2E


def _peer_route_body(sc_ref, tau_ref, a2_ref, b2_ref):
    t = sc_ref.shape[2]
    sub = lax.broadcasted_iota(jnp.int32, (8, t), 0)
    for h in range(PEER_HEADS):
        va = _top16_sorted(_key_groups(sc_ref, 2 * h))
        vb = _top16_sorted(_key_groups(sc_ref, 2 * h + 1))

        def stack8(rows):
            acc = jnp.zeros((8, t), F32)
            for i, rw in enumerate(rows):
                acc = jnp.where(sub == i, rw, acc)
            return acc

        def candidates(fa, fb, pad):
            fa_lo, fa_hi = stack8(fa[:8]), stack8(fa[8:])
            fb_hi = stack8(fb[8:])
            groups = [fa_lo + fb[0], fa_hi + fb[0], fa_lo + fb[1]]
            for qq, lim in ((2, 5), (3, 4), (4, 3), (5, 2), (6, 2), (7, 2)):
                groups.append(jnp.where(sub < lim, fa_lo + fb[qq], pad))
            groups.append(fa[0] + fb_hi)
            return groups

        cand = candidates(va, vb, NEG_INF)
        pad = [jnp.full((8, t), NEG_INF, F32)] * (PEER_TOPK - len(cand))
        best = _top16_sorted(cand + pad)
        tau = best[PEER_TOPK - 1]
        best0 = va[0] + vb[0]
        z = jnp.zeros((8, t), F32)
        for g in cand:
            z = z + jnp.where(g >= tau, jnp.exp(g - best0), 0.0)
        z = jnp.sum(z, axis=0, keepdims=True)
        log2z1 = jnp.log2(z) + 1.0
        a2_ref[h] = _shift_first(sc_ref[2 * h], va[0][0:1, :], log2z1)
        b2_ref[h] = _shift_second(sc_ref[2 * h + 1], vb[0][0:1, :])
        va2 = [_shift_first(v, va[0], log2z1) for v in va]
        vb2 = [_shift_second(v, vb[0]) for v in vb]
        cand2 = candidates(va2, vb2, 0.0)
        tau2 = jnp.full((8, t), jnp.inf, F32)
        for g, g2 in zip(cand, cand2):
            tau2 = jnp.minimum(tau2, jnp.where(g >= tau, g2, jnp.inf))
        tau_ref[h:h + 1, :] = jnp.min(tau2, axis=0, keepdims=True)


def peer_route(sc_t, tt=256):
    nhc, kk, n = sc_t.shape
    tt = min(tt, n)
    return pl.pallas_call(
        _peer_route_body,
        grid=(n // tt,),
        in_specs=[pl.BlockSpec((nhc, kk, tt), lambda i: (0, 0, i))],
        out_specs=[pl.BlockSpec((PEER_HEADS, tt), lambda i: (0, i)),
                   pl.BlockSpec((PEER_HEADS, kk, tt), lambda i: (0, 0, i)),
                   pl.BlockSpec((PEER_HEADS, kk, tt), lambda i: (0, 0, i))],
        out_shape=[jax.ShapeDtypeStruct((PEER_HEADS, n), F32),
                   jax.ShapeDtypeStruct((PEER_HEADS, kk, n), F32),
                   jax.ShapeDtypeStruct((PEER_HEADS, kk, n), F32)],
        compiler_params=_params("parallel"),
        name="peer_route",
    )(sc_t)


PEER_PAIR = 2 * PEER_KEYS


def _peer_mix_body(ht_ref, u_ref, vt_ref, a2_ref, b2_ref, tau_ref, x_ref, o_ref,
                   acc_ref, s0_ref, s1_ref, w0_ref, w1_ref, *, tb):
    c = pl.program_id(1)
    npair = u_ref.shape[0]
    tm = ht_ref.shape[1]
    ipp = PEER_PAIR // PEER_KEYS
    s_slots = (s0_ref, s1_ref)
    w_slots = (w0_ref, w1_ref)

    @pl.when(c == 0)
    def _():
        acc_ref[...] = jnp.zeros(acc_ref.shape, F32)

    def score(pair, slot):
        s_slots[slot][...] = jnp.dot(u_ref[pair], ht_ref[...],
                                     preferred_element_type=F32)

    def gate_act(pair, slot):
        for ii in range(ipp):
            i = (c * npair + pair) * ipp + ii
            a_rows = [a2_ref[h, pl.ds(i, 1), :] for h in range(PEER_HEADS)]
            rows = slice(ii * PEER_KEYS, (ii + 1) * PEER_KEYS)
            for t_i in range(tm // tb):
                ln = slice(t_i * tb, (t_i + 1) * tb)
                gate = jnp.zeros((PEER_KEYS, tb), F32)
                for h in range(PEER_HEADS):
                    s2 = a_rows[h][:, ln] + b2_ref[h, :, ln]
                    gate = gate + jnp.where(s2 >= tau_ref[h:h + 1, ln], jnp.exp2(s2), 0.0)
                s_blk = s_slots[slot][rows, ln]
                act = s_blk * (1.0 + lax.erf(s_blk * (2.0 ** -0.5)))
                w_slots[slot][rows, ln] = (act * gate).astype(BF16)

    def mix(pair, slot):
        acc_ref[...] += jnp.dot(vt_ref[pair], w_slots[slot][...],
                                preferred_element_type=F32)

    score(0, 0)

    def trip(q, carry):
        score(2 * q + 1, 1)
        gate_act(2 * q, 0)
        mix(2 * q, 0)
        score(jnp.minimum(2 * q + 2, npair - 1), 0)
        gate_act(2 * q + 1, 1)
        mix(2 * q + 1, 1)
        return carry

    lax.fori_loop(0, npair // 2, trip, 0)

    @pl.when(c == pl.num_programs(1) - 1)
    def _():
        o_ref[...] = x_ref[...] + acc_ref[...].T


def peer_mix(h_t, u3, vt3, a2, b2, tau2, x_res, tm=512, npair=8, tb=128):
    d, n = h_t.shape
    tm = min(tm, n)
    tb = min(tb, tm)
    assert u3.shape[0] % npair == 0 and npair % 2 == 0 and tm % tb == 0
    return pl.pallas_call(
        functools.partial(_peer_mix_body, tb=tb),
        grid=(n // tm, u3.shape[0] // npair),
        in_specs=[pl.BlockSpec((d, tm), lambda t, c: (0, t)),
                  pl.BlockSpec((npair, PEER_PAIR, d), lambda t, c: (c, 0, 0)),
                  pl.BlockSpec((npair, d, PEER_PAIR), lambda t, c: (c, 0, 0)),
                  pl.BlockSpec((PEER_HEADS, PEER_KEYS, tm), lambda t, c: (0, 0, t)),
                  pl.BlockSpec((PEER_HEADS, PEER_KEYS, tm), lambda t, c: (0, 0, t)),
                  pl.BlockSpec((PEER_HEADS, tm), lambda t, c: (0, t)),
                  pl.BlockSpec((tm, d), lambda t, c: (t, 0))],
        out_specs=pl.BlockSpec((tm, d), lambda t, c: (t, 0)),
        out_shape=jax.ShapeDtypeStruct((n, d), F32),
        scratch_shapes=[pltpu.VMEM((d, tm), F32),
                        pltpu.VMEM((PEER_PAIR, tm), F32), pltpu.VMEM((PEER_PAIR, tm), F32),
                        pltpu.VMEM((PEER_PAIR, tm), BF16), pltpu.VMEM((PEER_PAIR, tm), BF16)],
        compiler_params=_params("parallel", "arbitrary"),
        name="peer_mix",
    )(h_t, u3, vt3, a2, b2, tau2, x_res)


def kernel(x, positions, norm1_g, w_in, gate_b, diff_lam, subln_g, w_att_br, conv_w, conv_b,
           dt_bias, a_log, d_skip, ssm_norm_g, w_ssm_br, w_out, norm2_g, peer_wq, peer_keys,
           peer_u, peer_v, final_g):
    batch, seq, d = x.shape
    depth = w_in.shape[0]
    n = batch * seq
    qk_cols = 2 * ATT_HEADS * 2 * ATT_HEAD_DIM
    qkv_cols = 3 * ATT_HEADS * 2 * ATT_HEAD_DIM
    z_cols = SSM_HEADS * SSM_HEAD_DIM
    xbc_cols = z_cols + 2 * SSM_GROUPS * SSM_STATE
    main_cols = qkv_cols + z_cols + xbc_cols
    dt_cols = 2 * SSM_HEADS

    cosf, sin_a, sin_b = rotary_tables(positions)
    q_scale = ATT_HEAD_DIM ** -0.5 * LOG2E
    xf = x.reshape(n, d)

    for l in range(depth):
        lam_init = 0.8 - 0.6 * math.exp(-0.3 * l)
        w_l = w_in[l]
        w_main = jnp.concatenate([w_l[:, :qk_cols], w_l[:, qkv_cols:main_cols],
                                  w_l[:, qk_cols:qkv_cols]], axis=1).astype(BF16)
        w_dt = w_l[:, main_cols:main_cols + dt_cols].astype(BF16)
        w_gate = w_l[:, main_cols + dt_cols:].astype(BF16)

        h1 = rmsnorm(xf, norm1_g[l])
        proj = inproj_rotary(h1, w_main, cosf, sin_a, sin_b, q_scale)
        dt_raw = matmul(h1, w_dt, F32)
        gate_logits = matmul(h1, w_gate, F32)

        att = diff_attention(proj, (qk_cols + z_cols + xbc_cols) // LANES, diff_lam[l],
                             subln_g[l], lam_init, batch, seq)

        xc = conv_silu(proj, qk_cols + z_cols, conv_w[l], conv_b[l], batch, seq)
        y_f = ssd_scan(xc, dt_raw[:, :SSM_HEADS], dt_bias[l, 0], a_log[l, 0], batch, seq, False)
        y_b = ssd_scan(xc, dt_raw[:, SSM_HEADS:], dt_bias[l, 1], a_log[l, 1], batch, seq, True)
        m = gated_norm(y_f, y_b, xc, proj, qk_cols // z_cols, d_skip[l], ssm_norm_g[l])

        merged = branch_merge(att, m, w_att_br[l].astype(BF16), w_ssm_br[l].astype(BF16),
                              gate_logits, gate_b[l])
        xf = matmul(merged, w_out[l].astype(BF16), F32, residual=xf)

        h2 = rmsnorm(xf, norm2_g[l])
        keys = peer_keys[l].reshape(2 * PEER_HEADS, PEER_KEYS, -1).astype(BF16)
        sc_t = peer_scores(h2, peer_wq[l].astype(BF16), keys)
        tau2, a2, b2 = peer_route(sc_t)
        u3 = peer_u[l].astype(BF16).reshape(-1, PEER_PAIR, d)
        vt3 = peer_v[l].astype(BF16).reshape(-1, PEER_PAIR, d).transpose(0, 2, 1)
        xf = peer_mix(h2.T, u3, vt3, a2, b2, tau2, xf)

    return rmsnorm(xf, final_g, out_dtype=x.dtype).reshape(batch, seq, d)
```

```python
import functools
import math

import jax
import jax.numpy as jnp
from jax import lax
from jax.experimental import pallas as pl
from jax.experimental.pallas import tpu as pltpu

F32 = jnp.float32
BF16 = jnp.bfloat16

LANES = 128
VMEM_LIMIT = 48 * 1024 * 1024

ATT_HEADS = 8
ATT_HEAD_DIM = 64
ROPE_DIM = 16
ROPE_THETA = 500000.0
SSM_HEADS = 32
SSM_HEAD_DIM = 64
SSM_GROUPS = 8
SSM_STATE = 128
SSM_CONV = 5
SSM_CHUNK = 128
PEER_HEADS = 8
PEER_KEYS = 128
PEER_TOPK = 16
RMS_EPS = 1e-6
NEG_INF = float("-inf")
LOG2E = 1.4426950408889634


def _params(*sem, flags=None):
    return pltpu.CompilerParams(dimension_semantics=sem, vmem_limit_bytes=VMEM_LIMIT, flags=flags)


def _rmsnorm_body(x_ref, g_ref, o_ref):
    x = x_ref[...].astype(F32)
    ms = jnp.mean(x * x, axis=-1, keepdims=True)
    o_ref[...] = (x * lax.rsqrt(ms + RMS_EPS) * g_ref[...]).astype(o_ref.dtype)


def rmsnorm(x, g, out_dtype=BF16, tm=512):
    n, d = x.shape
    tm = min(tm, n)
    return pl.pallas_call(
        _rmsnorm_body,
        grid=(n // tm,),
        in_specs=[pl.BlockSpec((tm, d), lambda i: (i, 0)),
                  pl.BlockSpec((1, d), lambda i: (0, 0))],
        out_specs=pl.BlockSpec((tm, d), lambda i: (i, 0)),
        out_shape=jax.ShapeDtypeStruct((n, d), out_dtype),
        compiler_params=_params("parallel"),
        name="rmsnorm",
    )(x, g.reshape(1, d).astype(F32))


def _mm_body(a_ref, w_ref, o_ref):
    o_ref[...] = jnp.dot(a_ref[...], w_ref[...], preferred_element_type=F32).astype(o_ref.dtype)


def _mm_res_body(a_ref, w_ref, r_ref, o_ref):
    acc = jnp.dot(a_ref[...], w_ref[...], preferred_element_type=F32)
    o_ref[...] = (r_ref[...].astype(F32) + acc).astype(o_ref.dtype)


def matmul(a, w, out_dtype, residual=None, tm=512, tn=1024):
    m, k = a.shape
    n = w.shape[1]
    tm = min(tm, m)
    tn = min(tn, n)
    assert m % tm == 0 and n % tn == 0
    in_specs = [pl.BlockSpec((tm, k), lambda j, i: (i, 0)),
                pl.BlockSpec((k, tn), lambda j, i: (0, j))]
    args = [a, w]
    body = _mm_body
    if residual is not None:
        in_specs.append(pl.BlockSpec((tm, tn), lambda j, i: (i, j)))
        args.append(residual)
        body = _mm_res_body
    return pl.pallas_call(
        body,
        grid=(n // tn, m // tm),
        in_specs=in_specs,
        out_specs=pl.BlockSpec((tm, tn), lambda j, i: (i, j)),
        out_shape=jax.ShapeDtypeStruct((m, n), out_dtype),
        compiler_params=_params("parallel", "parallel"),
        name="matmul",
    )(*args)


def _inproj_body(a_ref, w_ref, cos_ref, sa_ref, sb_ref, o_ref, *, q_scale):
    j = pl.program_id(0)
    acc = jnp.dot(a_ref[...], w_ref[...], preferred_element_type=F32)

    @pl.when(j >= 2)
    def _():
        o_ref[...] = acc.astype(o_ref.dtype)

    @pl.when(j < 2)
    def _():
        scale = jnp.where(j == 0, q_scale, 1.0).astype(F32)
        cosf = cos_ref[...] * scale
        sa = sa_ref[...] * scale
        sb = sb_ref[...] * scale
        tn = acc.shape[1]
        for g in range(tn // LANES):
            t = acc[:, g * LANES:(g + 1) * LANES]
            half = ROPE_DIM // 2
            r = (t * cosf + pltpu.roll(t, half, axis=1) * sa
                 + pltpu.roll(t, LANES - half, axis=1) * sb)
            o_ref[:, g * LANES:(g + 1) * LANES] = r.astype(o_ref.dtype)


def inproj_rotary(h, w, cosf, sin_a, sin_b, q_scale, tm=512, tn=1024):
    m, k = h.shape
    n = w.shape[1]
    tm = min(tm, m)
    assert m % tm == 0 and n % tn == 0
    tab = pl.BlockSpec((tm, LANES), lambda j, i: (i, 0))
    return pl.pallas_call(
        functools.partial(_inproj_body, q_scale=q_scale),
        grid=(n // tn, m // tm),
        in_specs=[pl.BlockSpec((tm, k), lambda j, i: (i, 0)),
                  pl.BlockSpec((k, tn), lambda j, i: (0, j)),
                  tab, tab, tab],
        out_specs=pl.BlockSpec((tm, tn), lambda j, i: (i, j)),
        out_shape=jax.ShapeDtypeStruct((m, n), BF16),
        compiler_params=_params("parallel", "parallel"),
        name="inproj_rotary",
    )(h, w, cosf, sin_a, sin_b)


def rotary_tables(positions):
    half = ROPE_DIM // 2
    inv_freq = ROPE_THETA ** (-jnp.arange(0, ROPE_DIM, 2, dtype=F32) / ROPE_DIM)
    ang = positions.reshape(-1).astype(F32)[:, None] * inv_freq
    cos, sin = jnp.cos(ang), jnp.sin(ang)
    n = ang.shape[0]
    one = jnp.ones((n, ATT_HEAD_DIM - ROPE_DIM), F32)
    zero8 = jnp.zeros((n, half), F32)
    zero = jnp.zeros((n, ATT_HEAD_DIM - ROPE_DIM), F32)
    cos64 = jnp.concatenate([cos, cos, one], axis=1)
    sa64 = jnp.concatenate([zero8, sin, zero], axis=1)
    sb64 = jnp.concatenate([-sin, zero8, zero], axis=1)
    dup = lambda t: jnp.concatenate([t, t], axis=1)
    return dup(cos64), dup(sa64), dup(sb64)


def _attn_body(lam_ref, g_ref, q_ref, k_ref, v_ref, o_ref, qs_ref, m_ref, acc_ref, s_ref,
               *, tk, lam_init):
    tq = q_ref.shape[0]
    s_len = k_ref.shape[0]
    lane = lax.broadcasted_iota(jnp.int32, (1, LANES), 1)
    first = lane < ATT_HEAD_DIM
    q = q_ref[...]
    zero = jnp.zeros_like(q)
    qs_ref[0:tq, :] = jnp.where(first, q, zero)
    qs_ref[tq:2 * tq, :] = jnp.where(first, zero, q)
    m_ref[...] = jnp.full(m_ref.shape, NEG_INF, F32)
    acc_ref[...] = jnp.zeros(acc_ref.shape, F32)
    ones = jnp.ones((tk, LANES), BF16)

    def scores(kc, slot):
        off = pl.multiple_of(kc * tk, tk)
        s_ref[slot] = lax.dot_general(qs_ref[...], k_ref[pl.ds(off, tk), :],
                                      (((1,), (1,)), ((), ())),
                                      preferred_element_type=F32)

    def update(kc, slot):
        off = pl.multiple_of(kc * tk, tk)
        v_ext = jnp.concatenate([v_ref[pl.ds(off, tk), :], ones], axis=1)
        s = s_ref[slot]
        m_old = m_ref[...]
        m_new = jnp.maximum(m_old, jnp.max(s, axis=-1, keepdims=True))
        alpha = jnp.exp2(m_old - m_new)
        p = jnp.concatenate(
            [jnp.exp2(s[:, j * LANES:(j + 1) * LANES] - m_new) for j in range(tk // LANES)], axis=1)
        pv = jnp.dot(p.astype(BF16), v_ext, preferred_element_type=F32)
        acc_ref[...] = jnp.concatenate([alpha, alpha], axis=1) * acc_ref[...] + pv
        m_ref[...] = m_new

    n_chunks = s_len // tk
    assert n_chunks % 2 == 0
    scores(0, 0)

    def step(i, carry):
        scores(2 * i + 1, 1)
        update(2 * i, 0)
        scores(jnp.minimum(2 * i + 2, n_chunks - 1), 0)
        update(2 * i + 1, 1)
        return carry

    lax.fori_loop(0, n_chunks // 2, step, 0)

    lp = lam_ref[...].astype(F32)
    lam = (jnp.exp(jnp.sum(lp[0:1] * lp[1:2], axis=-1, keepdims=True))
           - jnp.exp(jnp.sum(lp[2:3] * lp[3:4], axis=-1, keepdims=True)) + lam_init)
    a1 = acc_ref[0:tq, :]
    a2 = acc_ref[tq:2 * tq, :]
    out = a1[:, :LANES] / a1[:, LANES:] - lam * (a2[:, :LANES] / a2[:, LANES:])
    ms = jnp.mean(out * out, axis=-1, keepdims=True)
    o_ref[...] = (out * lax.rsqrt(ms + RMS_EPS) * g_ref[...] * (1.0 - lam_init)).astype(o_ref.dtype)


def diff_attention(proj, v_blk, lam_params, subln_g, lam_init, batch, seq, tq=2048, tk=512):
    n = batch * seq
    tq = min(tq, seq)
    tk = min(tk, seq)
    nq = seq // tq
    h = ATT_HEADS
    return pl.pallas_call(
        functools.partial(_attn_body, tk=tk, lam_init=lam_init),
        grid=(batch, h, nq),
        in_specs=[pl.BlockSpec((4, ATT_HEAD_DIM), lambda b, hh, i: (0, 0)),
                  pl.BlockSpec((1, LANES), lambda b, hh, i: (0, 0)),
                  pl.BlockSpec((tq, LANES), lambda b, hh, i: (b * nq + i, hh)),
                  pl.BlockSpec((seq, LANES), lambda b, hh, i: (b, h + hh)),
                  pl.BlockSpec((seq, LANES), lambda b, hh, i: (b, v_blk + hh))],
        out_specs=pl.BlockSpec((tq, LANES), lambda b, hh, i: (b * nq + i, hh)),
        out_shape=jax.ShapeDtypeStruct((n, h * LANES), BF16),
        scratch_shapes=[pltpu.VMEM((2 * tq, LANES), BF16), pltpu.VMEM((2 * tq, LANES), F32),
                        pltpu.VMEM((2 * tq, 2 * LANES), F32),
                        pltpu.VMEM((2, 2 * tq, tk), F32)],
        compiler_params=_params("parallel", "parallel", "parallel"),
        name="diff_attention",
    )(lam_params.astype(F32), subln_g.reshape(1, LANES).astype(F32), proj, proj, proj)


HALO = 8


def _conv_body(prev_ref, cur_ref, next_ref, w_ref, b_ref, o_ref, *, n_seq_blocks):
    i = pl.program_id(1)
    ts = cur_ref.shape[0]
    prev = jnp.where(i > 0, prev_ref[...].astype(F32), 0.0)
    nxt = jnp.where(i < n_seq_blocks - 1, next_ref[...].astype(F32), 0.0)
    ext = jnp.concatenate([prev, cur_ref[...].astype(F32), nxt], axis=0)
    w = w_ref[...]
    acc = jnp.zeros(cur_ref.shape, F32) + b_ref[...]
    pad = SSM_CONV // 2
    for kk in range(SSM_CONV):
        start = HALO - pad + kk
        acc = acc + ext[start:start + ts, :] * w[kk:kk + 1, :]
    o_ref[...] = (acc * jax.nn.sigmoid(acc)).astype(o_ref.dtype)


def conv_silu(proj, col_off, conv_w, conv_b, batch, seq, ts=512, tc=512):
    n = batch * seq
    c = conv_w.shape[1]
    ts = min(ts, seq)
    nsb = seq // ts
    assert col_off % tc == 0 and c % tc == 0 and ts % HALO == 0
    cb = col_off // tc
    hb = ts // HALO
    last_halo = n // HALO - 1

    def prev_map(b, i, j):
        return (jnp.maximum((b * nsb + i) * hb - 1, 0), cb + j)

    def next_map(b, i, j):
        return (jnp.minimum((b * nsb + i + 1) * hb, last_halo), cb + j)

    return pl.pallas_call(
        functools.partial(_conv_body, n_seq_blocks=nsb),
        grid=(batch, nsb, c // tc),
        in_specs=[pl.BlockSpec((HALO, tc), prev_map),
                  pl.BlockSpec((ts, tc), lambda b, i, j: (b * nsb + i, cb + j)),
                  pl.BlockSpec((HALO, tc), next_map),
                  pl.BlockSpec((SSM_CONV, tc), lambda b, i, j: (0, j)),
                  pl.BlockSpec((1, tc), lambda b, i, j: (0, j))],
        out_specs=pl.BlockSpec((ts, tc), lambda b, i, j: (b * nsb + i, j)),
        out_shape=jax.ShapeDtypeStruct((n, c), BF16),
        compiler_params=_params("parallel", "parallel", "parallel"),
        name="conv_silu",
    )(proj, proj, proj, conv_w.astype(F32), conv_b.reshape(1, c).astype(F32))


def _softplus(x):
    return jnp.maximum(x, 0.0) + jnp.log1p(jnp.exp(-jnp.abs(x)))


def _ssd_body(x_ref, b_ref, c_ref, dt_ref, dtt_ref, bias_ref, biast_ref, alog_ref, alogt_ref,
              y_ref, state_ref, *, reverse):
    q = SSM_CHUNK
    hpg = SSM_HEADS // SSM_GROUPS
    gw = hpg * SSM_HEAD_DIM
    hi = lax.Precision.HIGHEST

    @pl.when(pl.program_id(1) == 0)
    def _():
        state_ref[...] = jnp.zeros(state_ref.shape, F32)

    row = lax.broadcasted_iota(jnp.int32, (q, q), 0)
    col = lax.broadcasted_iota(jnp.int32, (q, q), 1)
    keep = (col >= row) if reverse else (col <= row)
    incl = keep.astype(BF16)
    incl_t = ((row >= col) if reverse else (row <= col)).astype(BF16)

    def split(v):
        v_hi = v.astype(BF16)
        return v_hi, (v - v_hi.astype(F32)).astype(BF16)

    dt = _softplus(dt_ref[...] + bias_ref[...])
    dtt = _softplus(dtt_ref[...] + biast_ref[...])
    a = dt * (-jnp.exp(alog_ref[...]))
    at = dtt * (-jnp.exp(alogt_ref[...]))
    a_hi, a_lo = split(a)
    at_hi, at_lo = split(at)
    cum = (jnp.dot(incl, a_hi, preferred_element_type=F32)
           + jnp.dot(incl, a_lo, preferred_element_type=F32))
    cum_t = (jnp.dot(at_hi, incl_t, preferred_element_type=F32)
             + jnp.dot(at_lo, incl_t, preferred_element_type=F32))
    total = jnp.sum(a, axis=0, keepdims=True)

    hid = lax.broadcasted_iota(jnp.int32, (SSM_HEADS, SSM_HEADS * SSM_HEAD_DIM), 0)
    lid = lax.broadcasted_iota(jnp.int32, (SSM_HEADS, SSM_HEADS * SSM_HEAD_DIM), 1)
    expand = (lid // SSM_HEAD_DIM == hid).astype(BF16)

    def ex(v):
        v_hi, v_lo = split(v)
        return (jnp.dot(v_hi, expand, preferred_element_type=F32)
                + jnp.dot(v_lo, expand, preferred_element_type=F32))

    xdec = (x_ref[...].astype(F32) * ex(dt * jnp.exp(total - cum))).astype(BF16)
    dec_out = ex(jnp.exp(cum))
    dec_chunk = ex(jnp.exp(total))

    lane_g = lax.broadcasted_iota(jnp.int32, (1, gw), 1) // SSM_HEAD_DIM

    for g in range(SSM_GROUPS):
        bg = b_ref[:, g * SSM_STATE:(g + 1) * SSM_STATE]
        cg = c_ref[:, g * SSM_STATE:(g + 1) * SSM_STATE]
        cb = lax.dot_general(cg, bg, (((1,), (1,)), ((), ())), preferred_element_type=F32)
        xg = x_ref[:, g * gw:(g + 1) * gw]
        ms, xbd = [], []
        for hh in range(hpg):
            head = g * hpg + hh
            diff = cum[:, head:head + 1] - cum_t[head:head + 1, :]
            decay = jnp.exp(jnp.where(keep, diff, -1e30)) * dtt[head:head + 1, :]
            ms.append((cb * decay).astype(BF16))
            xbd.append(jnp.where(lane_g == hh, xg, jnp.zeros_like(xg)))
        m_cat = jnp.concatenate(ms, axis=1)
        x_bd = jnp.concatenate(xbd, axis=0)
        y_diag = jnp.dot(m_cat, x_bd, preferred_element_type=F32)

        st = state_ref[g]
        y_off = jnp.dot(cg, st.astype(BF16), preferred_element_type=F32) * dec_out[:, g * gw:(g + 1) * gw]
        y_ref[:, g * gw:(g + 1) * gw] = (y_diag + y_off).astype(y_ref.dtype)

        new = lax.dot_general(bg, xdec[:, g * gw:(g + 1) * gw], (((0,), (0,)), ((), ())),
                              preferred_element_type=F32)
        state_ref[g] = st * dec_chunk[:, g * gw:(g + 1) * gw] + new


def ssd_scan(xc, dt_raw, dt_bias, a_log, batch, seq, reverse):
    n = batch * seq
    q = SSM_CHUNK
    nc = seq // q
    hp = SSM_HEADS * SSM_HEAD_DIM
    gn = SSM_GROUPS * SSM_STATE
    assert xc.shape[1] == hp + 2 * gn and hp == 2 * gn

    def cidx(c):
        return nc - 1 - c if reverse else c

    return pl.pallas_call(
        functools.partial(_ssd_body, reverse=reverse),
        grid=(batch, nc),
        in_specs=[pl.BlockSpec((q, hp), lambda b, c: (b * nc + cidx(c), 0)),
                  pl.BlockSpec((q, gn), lambda b, c: (b * nc + cidx(c), 2)),
                  pl.BlockSpec((q, gn), lambda b, c: (b * nc + cidx(c), 3)),
                  pl.BlockSpec((q, SSM_HEADS), lambda b, c: (b * nc + cidx(c), 0)),
                  pl.BlockSpec((SSM_HEADS, q), lambda b, c: (0, b * nc + cidx(c))),
                  pl.BlockSpec((1, SSM_HEADS), lambda b, c: (0, 0)),
                  pl.BlockSpec((SSM_HEADS, 1), lambda b, c: (0, 0)),
                  pl.BlockSpec((1, SSM_HEADS), lambda b, c: (0, 0)),
                  pl.BlockSpec((SSM_HEADS, 1), lambda b, c: (0, 0))],
        out_specs=pl.BlockSpec((q, hp), lambda b, c: (b * nc + cidx(c), 0)),
        out_shape=jax.ShapeDtypeStruct((n, hp), BF16),
        scratch_shapes=[pltpu.VMEM((SSM_GROUPS, SSM_STATE, hp // SSM_GROUPS), F32)],
        compiler_params=_params("parallel", "arbitrary"),
        name="ssd_scan_bwd" if reverse else "ssd_scan_fwd",
    )(xc, xc, xc, dt_raw, dt_raw.T, dt_bias.reshape(1, -1).astype(F32),
      dt_bias.reshape(-1, 1).astype(F32), a_log.reshape(1, -1).astype(F32),
      a_log.reshape(-1, 1).astype(F32))


def _gated_norm_body(yf_ref, yb_ref, xs_ref, z_ref, d_ref, g_ref, o_ref):
    z = z_ref[...].astype(F32)
    y = (yf_ref[...].astype(F32) + yb_ref[...].astype(F32)
         + d_ref[...] * xs_ref[...].astype(F32)) * (z * jax.nn.sigmoid(z))
    ms = jnp.mean(y * y, axis=-1, keepdims=True)
    o_ref[...] = (y * lax.rsqrt(ms + RMS_EPS) * g_ref[...]).astype(o_ref.dtype)


def gated_norm(y_f, y_b, xc, proj, z_col_block, d_skip, norm_g, tm=256):
    n, c = y_f.shape
    tm = min(tm, n)
    row = lambda i: (i, 0)
    return pl.pallas_call(
        _gated_norm_body,
        grid=(n // tm,),
        in_specs=[pl.BlockSpec((tm, c), row), pl.BlockSpec((tm, c), row),
                  pl.BlockSpec((tm, c), row),
                  pl.BlockSpec((tm, c), lambda i: (i, z_col_block)),
                  pl.BlockSpec((1, c), lambda i: (0, 0)), pl.BlockSpec((1, c), lambda i: (0, 0))],
        out_specs=pl.BlockSpec((tm, c), row),
        out_shape=jax.ShapeDtypeStruct((n, c), BF16),
        compiler_params=_params("parallel"),
        name="gated_norm",
    )(y_f, y_b, xc, proj, jnp.repeat(d_skip.astype(F32), SSM_HEAD_DIM).reshape(1, c),
      norm_g.reshape(1, c).astype(F32))


def _merge_body(att_ref, m_ref, wa_ref, ws_ref, ga_ref, gs_ref, ba_ref, bs_ref, o_ref):
    a_out = jnp.dot(att_ref[...], wa_ref[...], preferred_element_type=F32)
    m_out = jnp.dot(m_ref[...], ws_ref[...], preferred_element_type=F32)
    g_att = jax.nn.sigmoid(ga_ref[...] + ba_ref[...])
    g_ssm = jax.nn.sigmoid(gs_ref[...] + bs_ref[...])
    o_ref[...] = (g_att * a_out + g_ssm * m_out).astype(o_ref.dtype)


def branch_merge(att, m, w_att, w_ssm, gate_logits, gate_b, tm=512, tn=512):
    n, d = att.shape[0], w_att.shape[1]
    tm = min(tm, n)
    nb = d // tn
    return pl.pallas_call(
        _merge_body,
        grid=(nb, n // tm),
        in_specs=[pl.BlockSpec((tm, att.shape[1]), lambda j, i: (i, 0)),
                  pl.BlockSpec((tm, m.shape[1]), lambda j, i: (i, 0)),
                  pl.BlockSpec((w_att.shape[0], tn), lambda j, i: (0, j)),
                  pl.BlockSpec((w_ssm.shape[0], tn), lambda j, i: (0, j)),
                  pl.BlockSpec((tm, tn), lambda j, i: (i, j)),
                  pl.BlockSpec((tm, tn), lambda j, i: (i, nb + j)),
                  pl.BlockSpec((1, tn), lambda j, i: (0, j)),
                  pl.BlockSpec((1, tn), lambda j, i: (0, nb + j))],
        out_specs=pl.BlockSpec((tm, tn), lambda j, i: (i, j)),
        out_shape=jax.ShapeDtypeStruct((n, d), BF16),
        compiler_params=_params("parallel", "parallel"),
        name="branch_merge",
    )(att, m, w_att, w_ssm, gate_logits, gate_logits, gate_b.reshape(1, -1).astype(F32),
      gate_b.reshape(1, -1).astype(F32))


def _tail_body(yf_ref, yb_ref, xs_ref, z_ref, d_ref, g_ref, att_ref, wa_ref, ws_ref, wo_ref,
               ga_ref, gs_ref, ba_ref, bs_ref, x_ref, o_ref):
    z = z_ref[...].astype(F32)
    y = (yf_ref[...].astype(F32) + yb_ref[...].astype(F32)
         + d_ref[...] * xs_ref[...].astype(F32)) * (z * jax.nn.sigmoid(z))
    ms = jnp.mean(y * y, axis=-1, keepdims=True)
    m = (y * lax.rsqrt(ms + RMS_EPS) * g_ref[...]).astype(BF16)
    a_out = jnp.dot(att_ref[...], wa_ref[...], preferred_element_type=F32)
    m_out = jnp.dot(m, ws_ref[...], preferred_element_type=F32)
    g_att = jax.nn.sigmoid(ga_ref[...] + ba_ref[...])
    g_ssm = jax.nn.sigmoid(gs_ref[...] + bs_ref[...])
    merged = (g_att * a_out + g_ssm * m_out).astype(BF16)
    o_ref[...] = x_ref[...] + jnp.dot(merged, wo_ref[...], preferred_element_type=F32)


def mixer_tail(y_f, y_b, xc, proj, z_col_block, d_skip, norm_g, att, w_att, w_ssm, w_out,
               gate_logits, gate_b, x_res, tm=256):
    n, c = y_f.shape
    d = w_out.shape[1]
    tm = min(tm, n)
    row = lambda i: (i, 0)
    full = lambda i: (0, 0)
    return pl.pallas_call(
        _tail_body,
        grid=(n // tm,),
        in_specs=[pl.BlockSpec((tm, c), row), pl.BlockSpec((tm, c), row), pl.BlockSpec((tm, c), row),
                  pl.BlockSpec((tm, c), lambda i: (i, z_col_block)),
                  pl.BlockSpec((1, c), full), pl.BlockSpec((1, c), full),
                  pl.BlockSpec((tm, att.shape[1]), row),
                  pl.BlockSpec(w_att.shape, full), pl.BlockSpec(w_ssm.shape, full),
                  pl.BlockSpec(w_out.shape, full),
                  pl.BlockSpec((tm, d), row), pl.BlockSpec((tm, d), lambda i: (i, 1)),
                  pl.BlockSpec((1, d), full), pl.BlockSpec((1, d), lambda i: (0, 1)),
                  pl.BlockSpec((tm, d), row)],
        out_specs=pl.BlockSpec((tm, d), row),
        out_shape=jax.ShapeDtypeStruct((n, d), F32),
        compiler_params=_params("parallel"),
        name="mixer_tail",
    )(y_f, y_b, xc, proj, jnp.repeat(d_skip.astype(F32), SSM_HEAD_DIM).reshape(1, c),
      norm_g.reshape(1, c).astype(F32), att, w_att, w_ssm, w_out, gate_logits, gate_logits,
      gate_b.reshape(1, -1).astype(F32), gate_b.reshape(1, -1).astype(F32), x_res)


def _peer_score_body(h_ref, wq_ref, keys_ref, o_ref):
    qry = jnp.dot(h_ref[...], wq_ref[...], preferred_element_type=F32).astype(BF16)
    for hc in range(2 * PEER_HEADS):
        qh = qry[:, hc * LANES:(hc + 1) * LANES]
        o_ref[hc] = lax.dot_general(keys_ref[hc], qh, (((1,), (1,)), ((), ())),
                                    preferred_element_type=F32)


def peer_scores(h, wq, keys, tm=512):
    n, d = h.shape
    tm = min(tm, n)
    nhc = 2 * PEER_HEADS
    return pl.pallas_call(
        _peer_score_body,
        grid=(n // tm,),
        in_specs=[pl.BlockSpec((tm, d), lambda i: (i, 0)),
                  pl.BlockSpec(wq.shape, lambda i: (0, 0)),
                  pl.BlockSpec(keys.shape, lambda i: (0, 0, 0))],
        out_specs=pl.BlockSpec((nhc, PEER_KEYS, tm), lambda i: (0, 0, i)),
        out_shape=jax.ShapeDtypeStruct((nhc, PEER_KEYS, n), F32),
        compiler_params=_params("parallel"),
        name="peer_scores",
    )(h, wq, keys)


def _batcher_pairs(n):
    pairs, p = [], 1
    while p < n:
        k = p
        while k >= 1:
            for j in range(k % p, n - k, 2 * k):
                for i in range(min(k, n - j - k)):
                    if (i + j) // (2 * p) == (i + j + k) // (2 * p):
                        pairs.append((i + j, i + j + k))
            k //= 2
        p *= 2
    return pairs


def _top16_sorted(v):
    n = PEER_TOPK
    v = list(v)
    assert len(v) == n

    def cmpx(i, j):
        v[i], v[j] = jnp.maximum(v[i], v[j]), jnp.minimum(v[i], v[j])

    for i, j in _batcher_pairs(n):
        cmpx(i, j)
    for shift in (4, 2, 1):
        other = [pltpu.roll(vk, shift, axis=0) for vk in v]
        v = [jnp.maximum(v[k], other[n - 1 - k]) for k in range(n)]
        d = n // 2
        while d >= 1:
            for k in range(n):
                if k & d == 0:
                    cmpx(k, k + d)
            d //= 2
    return v


def _key_groups(x_ref, idx):
    return [x_ref[idx, 8 * k:8 * k + 8, :] for k in range(PEER_KEYS // 8)]


def _shift_first(a, a_max, log2z1):
    return (a - a_max) * LOG2E - log2z1


def _shift_second(b, b_max):
    return (b - b_max) * LOG2E


def _peer_route_body(sc_ref, tau_ref, a2_ref, b2_ref):
    t = sc_ref.shape[2]
    sub = lax.broadcasted_iota(jnp.int32, (8, t), 0)
    for h in range(PEER_HEADS):
        va = _top16_sorted(_key_groups(sc_ref, 2 * h))
        vb = _top16_sorted(_key_groups(sc_ref, 2 * h + 1))

        def stack8(rows):
            acc = jnp.zeros((8, t), F32)
            for i, rw in enumerate(rows):
                acc = jnp.where(sub == i, rw, acc)
            return acc

        def candidates(fa, fb, pad):
            fa_lo, fa_hi = stack8(fa[:8]), stack8(fa[8:])
            fb_hi = stack8(fb[8:])
            groups = [fa_lo + fb[0], fa_hi + fb[0], fa_lo + fb[1]]
            for qq, lim in ((2, 5), (3, 4), (4, 3), (5, 2), (6, 2), (7, 2)):
                groups.append(jnp.where(sub < lim, fa_lo + fb[qq], pad))
            groups.append(fa[0] + fb_hi)
            return groups

        cand = candidates(va, vb, NEG_INF)
        pad = [jnp.full((8, t), NEG_INF, F32)] * (PEER_TOPK - len(cand))
        best = _top16_sorted(cand + pad)
        tau = best[PEER_TOPK - 1]
        best0 = va[0] + vb[0]
        z = jnp.zeros((8, t), F32)
        for g in cand:
            z = z + jnp.where(g >= tau, jnp.exp(g - best0), 0.0)
        z = jnp.sum(z, axis=0, keepdims=True)
        log2z1 = jnp.log2(z) + 1.0
        a2_ref[h] = _shift_first(sc_ref[2 * h], va[0][0:1, :], log2z1)
        b2_ref[h] = _shift_second(sc_ref[2 * h + 1], vb[0][0:1, :])
        va2 = [_shift_first(v, va[0], log2z1) for v in va]
        vb2 = [_shift_second(v, vb[0]) for v in vb]
        cand2 = candidates(va2, vb2, 0.0)
        tau2 = jnp.full((8, t), jnp.inf, F32)
        for g, g2 in zip(cand, cand2):
            tau2 = jnp.minimum(tau2, jnp.where(g >= tau, g2, jnp.inf))
        tau_ref[h:h + 1, :] = jnp.min(tau2, axis=0, keepdims=True)


def peer_route(sc_t, tt=256):
    nhc, kk, n = sc_t.shape
    tt = min(tt, n)
    return pl.pallas_call(
        _peer_route_body,
        grid=(n // tt,),
        in_specs=[pl.BlockSpec((nhc, kk, tt), lambda i: (0, 0, i))],
        out_specs=[pl.BlockSpec((PEER_HEADS, tt), lambda i: (0, i)),
                   pl.BlockSpec((PEER_HEADS, kk, tt), lambda i: (0, 0, i)),
                   pl.BlockSpec((PEER_HEADS, kk, tt), lambda i: (0, 0, i))],
        out_shape=[jax.ShapeDtypeStruct((PEER_HEADS, n), F32),
                   jax.ShapeDtypeStruct((PEER_HEADS, kk, n), F32),
                   jax.ShapeDtypeStruct((PEER_HEADS, kk, n), F32)],
        compiler_params=_params("parallel"),
        name="peer_route",
    )(sc_t)


PEER_PAIR = 2 * PEER_KEYS


def _peer_mix_body(ht_ref, u_ref, vt_ref, a2_ref, b2_ref, tau_ref, x_ref, o_ref,
                   acc_ref, s0_ref, s1_ref, w0_ref, w1_ref, *, tb):
    c = pl.program_id(1)
    npair = u_ref.shape[0]
    tm = ht_ref.shape[1]
    ipp = PEER_PAIR // PEER_KEYS
    s_slots = (s0_ref, s1_ref)
    w_slots = (w0_ref, w1_ref)

    @pl.when(c == 0)
    def _():
        acc_ref[...] = jnp.zeros(acc_ref.shape, F32)

    def score(pair, slot):
        s_slots[slot][...] = jnp.dot(u_ref[pair], ht_ref[...],
                                     preferred_element_type=F32)

    def gate_act(pair, slot):
        for ii in range(ipp):
            i = (c * npair + pair) * ipp + ii
            a_rows = [a2_ref[h, pl.ds(i, 1), :] for h in range(PEER_HEADS)]
            rows = slice(ii * PEER_KEYS, (ii + 1) * PEER_KEYS)
            for t_i in range(tm // tb):
                ln = slice(t_i * tb, (t_i + 1) * tb)
                gate = jnp.zeros((PEER_KEYS, tb), F32)
                for h in range(PEER_HEADS):
                    s2 = a_rows[h][:, ln] + b2_ref[h, :, ln]
                    gate = gate + jnp.where(s2 >= tau_ref[h:h + 1, ln], jnp.exp2(s2), 0.0)
                s_blk = s_slots[slot][rows, ln]
                act = s_blk * (1.0 + lax.erf(s_blk * (2.0 ** -0.5)))
                w_slots[slot][rows, ln] = (act * gate).astype(BF16)

    def mix(pair, slot):
        acc_ref[...] += jnp.dot(vt_ref[pair], w_slots[slot][...],
                                preferred_element_type=F32)

    score(0, 0)

    def trip(q, carry):
        score(2 * q + 1, 1)
        gate_act(2 * q, 0)
        mix(2 * q, 0)
        score(jnp.minimum(2 * q + 2, npair - 1), 0)
        gate_act(2 * q + 1, 1)
        mix(2 * q + 1, 1)
        return carry

    lax.fori_loop(0, npair // 2, trip, 0)

    @pl.when(c == pl.num_programs(1) - 1)
    def _():
        o_ref[...] = x_ref[...] + acc_ref[...].T


def peer_mix(h_t, u3, vt3, a2, b2, tau2, x_res, tm=512, npair=8, tb=128):
    d, n = h_t.shape
    tm = min(tm, n)
    tb = min(tb, tm)
    assert u3.shape[0] % npair == 0 and npair % 2 == 0 and tm % tb == 0
    return pl.pallas_call(
        functools.partial(_peer_mix_body, tb=tb),
        grid=(n // tm, u3.shape[0] // npair),
        in_specs=[pl.BlockSpec((d, tm), lambda t, c: (0, t)),
                  pl.BlockSpec((npair, PEER_PAIR, d), lambda t, c: (c, 0, 0)),
                  pl.BlockSpec((npair, d, PEER_PAIR), lambda t, c: (c, 0, 0)),
                  pl.BlockSpec((PEER_HEADS, PEER_KEYS, tm), lambda t, c: (0, 0, t)),
                  pl.BlockSpec((PEER_HEADS, PEER_KEYS, tm), lambda t, c: (0, 0, t)),
                  pl.BlockSpec((PEER_HEADS, tm), lambda t, c: (0, t)),
                  pl.BlockSpec((tm, d), lambda t, c: (t, 0))],
        out_specs=pl.BlockSpec((tm, d), lambda t, c: (t, 0)),
        out_shape=jax.ShapeDtypeStruct((n, d), F32),
        scratch_shapes=[pltpu.VMEM((d, tm), F32),
                        pltpu.VMEM((PEER_PAIR, tm), F32), pltpu.VMEM((PEER_PAIR, tm), F32),
                        pltpu.VMEM((PEER_PAIR, tm), BF16), pltpu.VMEM((PEER_PAIR, tm), BF16)],
        compiler_params=_params("parallel", "arbitrary"),
        name="peer_mix",
    )(h_t, u3, vt3, a2, b2, tau2, x_res)


def kernel(x, positions, norm1_g, w_in, gate_b, diff_lam, subln_g, w_att_br, conv_w, conv_b,
           dt_bias, a_log, d_skip, ssm_norm_g, w_ssm_br, w_out, norm2_g, peer_wq, peer_keys,
           peer_u, peer_v, final_g):
    batch, seq, d = x.shape
    depth = w_in.shape[0]
    n = batch * seq
    qk_cols = 2 * ATT_HEADS * 2 * ATT_HEAD_DIM
    qkv_cols = 3 * ATT_HEADS * 2 * ATT_HEAD_DIM
    z_cols = SSM_HEADS * SSM_HEAD_DIM
    xbc_cols = z_cols + 2 * SSM_GROUPS * SSM_STATE
    main_cols = qkv_cols + z_cols + xbc_cols
    dt_cols = 2 * SSM_HEADS

    cosf, sin_a, sin_b = rotary_tables(positions)
    q_scale = ATT_HEAD_DIM ** -0.5 * LOG2E
    xf = x.reshape(n, d)

    for l in range(depth):
        lam_init = 0.8 - 0.6 * math.exp(-0.3 * l)
        w_l = w_in[l]
        w_main = jnp.concatenate([w_l[:, :qk_cols], w_l[:, qkv_cols:main_cols],
                                  w_l[:, qk_cols:qkv_cols]], axis=1).astype(BF16)
        w_dt = w_l[:, main_cols:main_cols + dt_cols].astype(BF16)
        w_gate = w_l[:, main_cols + dt_cols:].astype(BF16)

        h1 = rmsnorm(xf, norm1_g[l])
        proj = inproj_rotary(h1, w_main, cosf, sin_a, sin_b, q_scale)
        dt_raw = matmul(h1, w_dt, F32)
        gate_logits = matmul(h1, w_gate, F32)

        att = diff_attention(proj, (qk_cols + z_cols + xbc_cols) // LANES, diff_lam[l],
                             subln_g[l], lam_init, batch, seq)

        xc = conv_silu(proj, qk_cols + z_cols, conv_w[l], conv_b[l], batch, seq)
        y_f = ssd_scan(xc, dt_raw[:, :SSM_HEADS], dt_bias[l, 0], a_log[l, 0], batch, seq, False)
        y_b = ssd_scan(xc, dt_raw[:, SSM_HEADS:], dt_bias[l, 1], a_log[l, 1], batch, seq, True)
        xf = mixer_tail(y_f, y_b, xc, proj, qk_cols // z_cols, d_skip[l], ssm_norm_g[l], att,
                        w_att_br[l].astype(BF16), w_ssm_br[l].astype(BF16), w_out[l].astype(BF16),
                        gate_logits, gate_b[l], xf)

        h2 = rmsnorm(xf, norm2_g[l])
        keys = peer_keys[l].reshape(2 * PEER_HEADS, PEER_KEYS, -1).astype(BF16)
        sc_t = peer_scores(h2, peer_wq[l].astype(BF16), keys)
        tau2, a2, b2 = peer_route(sc_t)
        u3 = peer_u[l].astype(BF16).reshape(-1, PEER_PAIR, d)
        vt3 = peer_v[l].astype(BF16).reshape(-1, PEER_PAIR, d).transpose(0, 2, 1)
        xf = peer_mix(h2.T, u3, vt3, a2, b2, tau2, xf)

    return rmsnorm(xf, final_g, out_dtype=x.dtype).reshape(batch, seq, d)
```
